```python
import math
import jax, jax.numpy as jnp
from jax import lax
import numpy as np

D_MODEL = 1024
BATCH = 8
SEQ = 4096
DEPTH = 4

GRID_W = 64
CTX_LEN = 256
BR_W = 512
D_A = BR_W
N_A = 64
H_A = D_A // N_A
R_DECAY = 64
R_ICLR = 64
GN_EPS = 64e-5
D_B = BR_W
CONV_K = 3
H_C = 4
DH_C = 64
DV_C = 2 * DH_C
D_QC = H_C * 2 * DH_C
D_C = H_C * DV_C
Q_BLOCK = 128
ROPE_BASE = 10000.0
N_BRANCH = 3
RMS_EPS = 1e-6
IN_WIDTHS = (D_A, D_A, D_A, R_DECAY, R_DECAY, R_ICLR, R_ICLR, D_A,
             D_B, D_B, D_B, D_B,
             D_QC, D_QC, D_C, D_C,
             N_BRANCH * D_MODEL)
N_IN = 4 * D_A + 2 * R_DECAY + 2 * R_ICLR + 4 * D_B + 2 * D_QC + 2 * D_C + N_BRANCH * D_MODEL

kernel_name = "hybrid_rwkv7_shortconv_diffattn_prefix_dit"


def rms_norm(x, g):
    xf = x.astype(jnp.float32)
    y = xf * lax.rsqrt(jnp.mean(xf * xf, axis=-1, keepdims=True) + RMS_EPS)
    return (y * g.astype(jnp.float32)).astype(x.dtype)


def heads(t, n):
    return t.reshape(t.shape[:-1] + (n, t.shape[-1] // n))


def split_columns(p):
    offsets = [int(o) for o in np.cumsum(IN_WIDTHS)[:-1]]
    return jnp.split(p, offsets, axis=-1)


def adaln_modulate(x, g, cond, w_mod, b_mod):
    mod = jax.nn.silu(cond) @ w_mod + b_mod
    shift, scale, gate = jnp.split(mod, 3, axis=-1)
    return rms_norm(x, g) * (1 + scale) + shift, gate


def axial_rope_angles(n_tokens):
    rows = n_tokens // GRID_W
    row = jnp.repeat(jnp.arange(rows), GRID_W).astype(jnp.float32)
    col = jnp.tile(jnp.arange(GRID_W), rows).astype(jnp.float32)
    n_freq = DH_C // 4
    inv_freq = ROPE_BASE ** (-jnp.arange(n_freq, dtype=jnp.float32) / n_freq)
    return jnp.concatenate([row[:, None] * inv_freq, col[:, None] * inv_freq], axis=-1)


def apply_rope(x, ang):
    n_tok = ang.shape[0]
    cos = jnp.cos(ang).reshape(n_tok, 2, DH_C // 4)[None, :, None, None].astype(x.dtype)
    sin = jnp.sin(ang).reshape(n_tok, 2, DH_C // 4)[None, :, None, None].astype(x.dtype)
    xr = x.reshape(x.shape[:-1] + (2, 2, DH_C // 4))
    x1, x2 = xr[..., 0, :], xr[..., 1, :]
    out = jnp.stack([x1 * cos - x2 * sin, x2 * cos + x1 * sin], axis=-2)
    return out.reshape(x.shape)


def rwkv_streams(r, k, v, lw, la, decay_w0, decay_up, iclr_a0, iclr_up, k_k, k_a):
    kk = heads(k * k_k, H_A).astype(jnp.float32)
    kk = kk * lax.rsqrt(jnp.sum(kk * kk, axis=-1, keepdims=True) + 1e-12)
    dirs = []
    for d in range(2):
        w_raw = (decay_w0[d] + jnp.tanh(lw[d]) @ decay_up[d]).astype(jnp.float32)
        log_w = -jnp.exp(-jax.nn.softplus(-w_raw) - 0.5)
        a = jax.nn.sigmoid(iclr_a0[d] + la[d] @ iclr_up[d])
        k_dir = k * (1 + (a - 1) * k_a)
        dirs.append((heads(jnp.exp(log_w), H_A), heads(k_dir, H_A), heads(a, H_A) * kk))
    return heads(r, H_A), heads(k, H_A), heads(v, H_A), kk, dirs


def rwkv7_scan(r, v, kk, decay, k_dir, akk, state0, reverse):
    xs = tuple(jnp.swapaxes(t.astype(jnp.float32), 0, 1) for t in (r, v, kk, decay, k_dir, akk))

    def step(s, inp):
        r_t, v_t, kk_t, w_t, k_t, akk_t = inp
        s_kk = jnp.einsum('bhvk,bhk->bhv', s, kk_t)
        s = s * w_t[:, :, None, :] - s_kk[..., None] * akk_t[:, :, None, :] + v_t[..., None] * k_t[:, :, None, :]
        return s, jnp.einsum('bhvk,bhk->bhv', s, r_t)

    s_fin, ys = lax.scan(step, state0, xs, reverse=reverse)
    return jnp.swapaxes(ys, 0, 1).astype(r.dtype), s_fin


def rwkv_output(y, r, k, v, r_k, gn_g, gn_b):
    yf = y.astype(jnp.float32)
    mu = jnp.mean(yf, axis=-1, keepdims=True)
    var = jnp.mean(jnp.square(yf - mu), axis=-1, keepdims=True)
    yn = ((yf - mu) * lax.rsqrt(var + GN_EPS)).astype(y.dtype)
    bonus = jnp.sum(r * k * r_k, axis=-1, keepdims=True) * v
    flat = lambda t: t.reshape(t.shape[:-2] + (D_A,))
    return flat(yn) * gn_g + gn_b + flat(bonus)


def short_conv(u, conv_w, conv_b):
    n_tok = u.shape[1]
    up = jnp.pad(u, ((0, 0), (1, 1), (0, 0)))
    return up[:, :n_tok] * conv_w[0] + up[:, 1:n_tok + 1] * conv_w[1] + up[:, 2:] * conv_w[2] + conv_b


def qk_heads(t):
    return t.reshape(t.shape[:-1] + (H_C, 2, DH_C))


def diff_attend(q, k, v, lam):
    s = jnp.einsum('bqhcd,bkhcd->bhcqk', q, k).astype(jnp.float32) * (DH_C ** -0.5)
    p = jax.nn.softmax(s, axis=-1)
    attn = p[:, :, 0] - lam * p[:, :, 1]
    return jnp.einsum('bhqk,bkhv->bqhv', attn.astype(v.dtype), v)


def diff_out(o, subln_g, lam_init):
    o = rms_norm(o, subln_g) * (1 - lam_init)
    return o.reshape(o.shape[:-2] + (D_C,))


def merge_branches(y_a, y_b, y_c, z_a, z_b, z_c, g_merge, w_branch, w_out):
    y = jnp.stack([y_a * jax.nn.silu(z_a), y_b * jax.nn.silu(z_b), y_c * jax.nn.silu(z_c)], axis=-2)
    br = jnp.einsum('btiw,iwd->btid', y, w_branch)
    gates = jax.nn.sigmoid(g_merge.reshape(g_merge.shape[:-1] + (N_BRANCH, D_MODEL)))
    return jnp.sum(gates * br, axis=-2) @ w_out


def hybrid_layer(x, ctx, c, c_ctx, lam_init, update_ctx, w_mod, b_mod, norm_g, w_in,
                 decay_w0, decay_up, iclr_a0, iclr_up, k_k, k_a, r_k, gn_g, gn_b,
                 conv_w, conv_b, qk_norm_g, lambda_qk, subln_g, w_branch, w_out):
    bsz, n_tok, _ = x.shape
    h, gate = adaln_modulate(x, norm_g, c[:, None, :], w_mod, b_mod)
    hc, gate_c = adaln_modulate(ctx, norm_g, c_ctx[None, None, :], w_mod, b_mod)
    (r, k, v, lwf, lwb, laf, lab, z_a, gb, gc, u, z_b, qd, kd, vd, z_c, g_merge) = split_columns(h @ w_in)
    (r_c, k_c, v_c, lwf_c, lwb_c, laf_c, lab_c, z_a_c, gb_c, gc_c, u_c, z_b_c,
     qd_c, kd_c, vd_c, z_c_c, g_merge_c) = split_columns(hc @ w_in)

    rh, kh, vh, kk, dirs = rwkv_streams(r, k, v, (lwf, lwb), (laf, lab), decay_w0, decay_up, iclr_a0, iclr_up, k_k, k_a)
    rh_c, kh_c, vh_c, kk_c, dirs_c = rwkv_streams(r_c, k_c, v_c, (lwf_c, lwb_c), (laf_c, lab_c),
                                                  decay_w0, decay_up, iclr_a0, iclr_up, k_k, k_a)
    s0 = jnp.zeros((bsz, H_A, N_A, N_A), jnp.float32)
    yc_f, sc_f = rwkv7_scan(rh_c, vh_c, kk_c, *dirs_c[0], s0, False)
    yc_b, sc_b = rwkv7_scan(rh_c, vh_c, kk_c, *dirs_c[1], s0, True)
    y_f, _ = rwkv7_scan(rh, vh, kk, *dirs[0], sc_f, False)
    y_b, _ = rwkv7_scan(rh, vh, kk, *dirs[1], sc_b, True)
    y_a = rwkv_output(y_f + y_b, rh, kh, vh, r_k, gn_g, gn_b)

    y_bc = gb * short_conv(gc * u, conv_w, conv_b)

    lam = (jnp.exp(jnp.sum(lambda_qk[0] * lambda_qk[1]).astype(jnp.float32))
           - jnp.exp(jnp.sum(lambda_qk[2] * lambda_qk[3]).astype(jnp.float32)) + lam_init)
    ang = axial_rope_angles(n_tok)
    q = apply_rope(rms_norm(qk_heads(qd), qk_norm_g[0]), ang)
    kl = apply_rope(rms_norm(qk_heads(kd), qk_norm_g[1]), ang)
    kc_att = rms_norm(qk_heads(kd_c), qk_norm_g[1])
    vc_att = heads(vd_c, H_C)
    k_all = jnp.concatenate([kl, kc_att], axis=1)
    v_all = jnp.concatenate([heads(vd, H_C), vc_att], axis=1)
    n_blk = n_tok // Q_BLOCK
    qb = jnp.swapaxes(q.reshape(bsz, n_blk, Q_BLOCK, H_C, 2, DH_C), 0, 1)
    ob = lax.map(lambda qi: diff_attend(qi, k_all, v_all, lam), qb)
    o = jnp.swapaxes(ob, 0, 1).reshape(bsz, n_tok, H_C, DV_C)
    y_c = diff_out(o, subln_g, lam_init)

    x = x + gate * merge_branches(y_a, y_bc, y_c, z_a, z_b, z_c, g_merge, w_branch, w_out)

    if update_ctx:
        y_a_c = rwkv_output(yc_f + yc_b, rh_c, kh_c, vh_c, r_k, gn_g, gn_b)
        y_b_c = gb_c * short_conv(gc_c * u_c, conv_w, conv_b)
        q_c = rms_norm(qk_heads(qd_c), qk_norm_g[0])
        y_c_c = diff_out(diff_attend(q_c, kc_att, vc_att, lam), subln_g, lam_init)
        ctx = ctx + gate_c * merge_branches(y_a_c, y_b_c, y_c_c, z_a_c, z_b_c, z_c_c, g_merge_c, w_branch, w_out)
    return x, ctx


def setup_inputs(seed: int = 0) -> dict:
    key = jax.random.key(seed)
    ks = jax.random.split(key, 26)
    L, D = DEPTH, D_MODEL

    def nrm(k, shape, scale):
        return jax.random.normal(k, shape, jnp.float32) * scale

    return {
        "x": nrm(ks[0], (BATCH, SEQ, D), 1.0),
        "c": nrm(ks[1], (BATCH, D), 1.0),
        "ctx": nrm(ks[2], (BATCH, CTX_LEN, D), 1.0),
        "c_ctx": nrm(ks[3], (D,), 1.0),
        "w_mod": nrm(ks[4], (L, D, 3 * D), 0.5 * D ** -0.5),
        "b_mod": nrm(ks[5], (L, 3 * D), 0.02),
        "norm_g": 1.0 + nrm(ks[6], (L, D), 0.02),
        "w_in": nrm(ks[7], (L, D, N_IN), D ** -0.5),
        "decay_w0": jax.random.uniform(ks[8], (L, 2, D_A), jnp.float32, -5.0, 1.0),
        "decay_up": nrm(ks[9], (L, 2, R_DECAY, D_A), 0.1 * R_DECAY ** -0.5),
        "iclr_a0": nrm(ks[10], (L, 2, D_A), 0.5),
        "iclr_up": nrm(ks[11], (L, 2, R_ICLR, D_A), 0.1 * R_ICLR ** -0.5),
        "k_k": 0.85 + nrm(ks[12], (L, D_A), 0.05),
        "k_a": 1.0 + nrm(ks[13], (L, D_A), 0.05),
        "r_k": nrm(ks[14], (L, H_A, N_A), 0.1),
        "gn_g": 1.0 + nrm(ks[15], (L, D_A), 0.02),
        "gn_b": nrm(ks[16], (L, D_A), 0.02),
        "conv_w": nrm(ks[17], (L, CONV_K, D_B), CONV_K ** -0.5),
        "conv_b": nrm(ks[18], (L, D_B), 0.02),
        "qk_norm_g": 1.0 + nrm(ks[19], (L, 2, DH_C), 0.02),
        "lambda_qk": nrm(ks[20], (L, 4, DH_C), 0.1),
        "subln_g": 1.0 + nrm(ks[21], (L, DV_C), 0.02),
        "w_branch": nrm(ks[22], (L, N_BRANCH, BR_W, D), BR_W ** -0.5),
        "w_out": nrm(ks[23], (L, D, D), D ** -0.5),
    }


def reference(x, c, ctx, c_ctx, w_mod, b_mod, norm_g, w_in, decay_w0, decay_up, iclr_a0, iclr_up,
              k_k, k_a, r_k, gn_g, gn_b, conv_w, conv_b, qk_norm_g, lambda_qk, subln_g, w_branch, w_out):
    for l in range(DEPTH):
        lam_init = 0.8 - 0.6 * math.exp(-0.3 * l)
        x, ctx = hybrid_layer(x, ctx, c, c_ctx, lam_init, l < DEPTH - 1,
                              w_mod[l], b_mod[l], norm_g[l], w_in[l],
                              decay_w0[l], decay_up[l], iclr_a0[l], iclr_up[l], k_k[l], k_a[l], r_k[l],
                              gn_g[l], gn_b[l], conv_w[l], conv_b[l], qk_norm_g[l], lambda_qk[l],
                              subln_g[l], w_branch[l], w_out[l])
    return x
```

```python
import functools
import math

import jax
import jax.numpy as jnp
from jax import lax
from jax.experimental import pallas as pl
from jax.experimental.pallas import tpu as pltpu

F32 = jnp.float32
BF16 = jnp.bfloat16

ROW_TILE = 256
CHUNK = 64
HEAD_A = 64
HALF_W = 256
HEADS_PER_HALF = HALF_W // HEAD_A
DH_C = 64
DV_C = 128
GRID_W = 64
ROPE_BASE = 10000.0
RMS_EPS = 1e-6
GN_EPS = 64e-5
KK_EPS = 1e-12
EXP_M05 = math.exp(-0.5)
VMEM_LIMIT = 56 * 1024 * 1024

NT_DIMS = (((1,), (1,)), ((), ()))
TN_DIMS = (((0,), (0,)), ((), ()))

COL_R, COL_K, COL_V, COL_ZA, COL_GB, COL_GC, COL_U, COL_ZB, COL_QD, COL_KD, COL_VD, COL_ZC = range(12)


def _dot(a, b):
    return jnp.dot(a, b, preferred_element_type=F32)


def _split_bf16(x):
    hi = x.astype(BF16)
    lo = (x - hi.astype(F32)).astype(BF16)
    return hi, lo


def _dot_exact_rhs(x, w_bf16):
    hi, lo = _split_bf16(x)
    return _dot(hi, w_bf16) + _dot(lo, w_bf16)


def _dot_exact_lhs(w_bf16, x):
    hi, lo = _split_bf16(x)
    return _dot(w_bf16, hi) + _dot(w_bf16, lo)


def _dot3(a, b):
    ah, al = _split_bf16(a)
    bh, bl = _split_bf16(b)
    return _dot(ah, bh) + _dot(ah, bl) + _dot(al, bh)


def _sigmoid(x):
    return 1.0 / (1.0 + jnp.exp(-x))


def _mod_kernel(cond_ref, w_ref, b_ref, o_ref):
    cnd = cond_ref[...]
    o_ref[0] = _dot3(cnd * _sigmoid(cnd), w_ref[0]) + b_ref[0]


def _modulation(cond, w_mod, b_mod):
    depth, d, d3 = w_mod.shape
    n = cond.shape[0]
    return pl.pallas_call(
        _mod_kernel,
        grid=(depth, d3 // d),
        in_specs=[pl.BlockSpec((n, d), lambda l, j: (0, 0)),
                  pl.BlockSpec((1, d, d), lambda l, j: (l, 0, j)),
                  pl.BlockSpec((1, 1, d), lambda l, j: (l, 0, j))],
        out_specs=pl.BlockSpec((1, n, d), lambda l, j: (l, 0, j)),
        out_shape=jax.ShapeDtypeStruct((depth, n, d3), F32),
        name="modulation",
    )(cond, w_mod, b_mod.reshape(depth, 1, d3))


def _in_kernel(x_ref, mod_ref, g_ref, w_ref, o_ref):
    d = x_ref.shape[1]
    x = x_ref[...]
    ms = jnp.mean(x * x, axis=-1, keepdims=True)
    y = x * lax.rsqrt(ms + RMS_EPS) * g_ref[...]
    m = mod_ref[0]
    h = y * (1.0 + m[:, d:2 * d]) + m[:, 0:d]
    o_ref[...] = _dot(h.astype(BF16), w_ref[...]).astype(BF16)


def _in_proj(xs, mod_l, norm_g, w_in, mod_row):
    m, d = xs.shape
    n = w_in.shape[1]
    n_col = 2
    tn = n // n_col
    return pl.pallas_call(
        _in_kernel,
        grid=(n_col, m // ROW_TILE),
        in_specs=[pl.BlockSpec((ROW_TILE, d), lambda j, i: (i, 0)),
                  pl.BlockSpec((1, 1, 3 * d), lambda j, i: (mod_row(i), 0, 0)),
                  pl.BlockSpec((1, d), lambda j, i: (0, 0)),
                  pl.BlockSpec((d, tn), lambda j, i: (0, j))],
        out_specs=pl.BlockSpec((ROW_TILE, tn), lambda j, i: (i, j)),
        out_shape=jax.ShapeDtypeStruct((m, n), BF16),
        compiler_params=pltpu.CompilerParams(vmem_limit_bytes=VMEM_LIMIT),
        name="in_proj",
    )(xs, mod_l, norm_g, w_in)


def _prep_kernel(q_ref, k_ref, kr_ref, cos_ref, sin_ref, gq_ref, gk_ref, kkw_ref, ones_ref,
                 qo_ref, ko_ref, kko_ref):
    ones = ones_ref[...]
    cos = cos_ref[...]
    sin = sin_ref[...]
    w = q_ref.shape[1]
    lane = lax.broadcasted_iota(jnp.int32, q_ref.shape, 1)
    first = (lane & 31) < 16

    def norm_rope(x, g):
        ms = _dot_exact_rhs(x * x, ones) * (1.0 / DH_C)
        xn = x * lax.rsqrt(ms + RMS_EPS) * g
        sw = jnp.where(first, pltpu.roll(xn, w - 16, 1), pltpu.roll(xn, 16, 1))
        return xn * cos + sw * sin

    qo_ref[...] = (norm_rope(q_ref[...].astype(F32), gq_ref[...]) * (DH_C ** -0.5)).astype(BF16)
    ko_ref[...] = norm_rope(k_ref[...].astype(F32), gk_ref[...]).astype(BF16)
    kr = kr_ref[...].astype(F32) * kkw_ref[...]
    kko_ref[...] = (kr * lax.rsqrt(_dot_exact_rhs(kr * kr, ones) + KK_EPS)).astype(BF16)


def _prep(p, cos_t, sin_t, gq, gk, k_k, ones_bd, tiles_per_batch):
    m = p.shape[0]
    w = 512
    col = lambda c: pl.BlockSpec((ROW_TILE, w), lambda i, c=c: (i, c))
    par = pl.BlockSpec((1, w), lambda i: (0, 0))
    tab = pl.BlockSpec((ROW_TILE, w), lambda i: (i % tiles_per_batch, 0))
    out = pl.BlockSpec((ROW_TILE, w), lambda i: (i, 0))
    return pl.pallas_call(
        _prep_kernel,
        grid=(m // ROW_TILE,),
        in_specs=[col(COL_QD), col(COL_KD), col(COL_K), tab, tab, par, par, par,
                  pl.BlockSpec((w, w), lambda i: (0, 0))],
        out_specs=[out, out, out],
        out_shape=[jax.ShapeDtypeStruct((m, w), BF16)] * 3,
        name="prep",
    )(p, p, p, cos_t, sin_t, gq, gk, k_k, ones_bd)


def _expand(z, same_head):
    zz = jnp.concatenate([z] * HEADS_PER_HALF, axis=0)
    return jnp.where(same_head, zz, 0.0).astype(BF16)


def _rwkv_chunk(r, k, v, kk, logw, a, k_a, h0, reverse, masks):
    same_head, strict, incl, eye, ltri = masks
    cum = _dot_exact_lhs(ltri, logw)
    cumx = cum - logw
    cum_end = cum[0:1] if reverse else cum[CHUNK - 1:CHUNK]
    kdir = k * (1.0 + (a - 1.0) * k_a)
    akk = a * kk
    en = jnp.exp(-cum)
    ec = jnp.exp(cum_end - cum)
    xk = _expand(kk * jnp.exp(cumx), same_head)
    xr_f = jnp.where(same_head, jnp.concatenate([r * jnp.exp(cum)] * HEADS_PER_HALF, axis=0), 0.0)
    xr = xr_f.astype(BF16)
    ya = _expand(akk * en, same_head)
    yk = _expand(kdir * en, same_head)
    ab = _expand(akk * ec, same_head)
    kb = _expand(kdir * ec, same_head)
    vx = _expand(v, same_head)
    p_end = jnp.exp(cum_end)

    ntd = lambda x, y: lax.dot_general(x, y, NT_DIMS, preferred_element_type=F32)
    tnd = lambda x, y: lax.dot_general(x, y, TN_DIMS, preferred_element_type=F32)
    n = jnp.where(strict, ntd(xk, ya), 0.0)
    a_kk = jnp.where(strict, ntd(xk, yk), 0.0).astype(BF16)
    b_rk = jnp.where(incl, ntd(xr, yk), 0.0).astype(BF16)
    b_ra = jnp.where(incl, ntd(xr, ya), 0.0).astype(BF16)
    t = jnp.where(eye, 1.0, 0.0) - n
    npow = n
    for _ in range(5):
        nb = npow.astype(BF16)
        npow = _dot(nb, nb)
        t = t + _dot(t.astype(BF16), npow.astype(BF16))
    tb = t.astype(BF16)
    w1 = _dot(a_kk, vx)
    khat = _dot(tb, xk).astype(BF16)
    uhat = _dot(tb, w1.astype(BF16)).astype(BF16)
    mx = jnp.where(eye, jnp.broadcast_to(p_end, (HALF_W, HALF_W)), 0.0) - tnd(ab, khat)
    gx = tnd(kb, vx) - tnd(ab, uhat)
    rhat = xr_f - _dot(b_ra, khat)
    yhat = _dot(b_rk, vx) - _dot(b_ra, uhat)
    h0b = h0.astype(BF16)
    y = _dot(rhat.astype(BF16), h0b) + yhat
    h1 = _dot(mx.astype(BF16), h0b) + gx
    y_nat = y[0:CHUNK]
    for hd in range(1, HEADS_PER_HALF):
        y_nat = y_nat + y[hd * CHUNK:(hd + 1) * CHUNK]
    return y_nat, h1


def _rwkv_kernel(rkvf_ref, kkf_ref, llf_ref, rkvb_ref, kkb_ref, llb_ref,
                 w0_ref, dup_ref, a0_ref, aup_ref, ka_ref, yf_ref, yb_ref, h_ref):
    @pl.when(pl.program_id(1) == 0)
    def _():
        h_ref[...] = jnp.zeros(h_ref.shape, F32)

    rr = lax.broadcasted_iota(jnp.int32, (HALF_W, HALF_W), 0)
    cc = lax.broadcasted_iota(jnp.int32, (HALF_W, HALF_W), 1)
    same_head = (rr >> 6) == (cc >> 6)
    tt, ss = rr & (CHUNK - 1), cc & (CHUNK - 1)
    eye = rr == cc
    ti = lax.broadcasted_iota(jnp.int32, (CHUNK, CHUNK), 0)
    si = lax.broadcasted_iota(jnp.int32, (CHUNK, CHUNK), 1)
    d_model_a = rkvf_ref.shape[1] // 3
    k_a = ka_ref[...]

    for d, (rkv_ref, kk_ref, ll_ref, y_ref) in enumerate(
            ((rkvf_ref, kkf_ref, llf_ref, yf_ref), (rkvb_ref, kkb_ref, llb_ref, yb_ref))):
        reverse = d == 1
        strict = (ss > tt) if reverse else (ss < tt)
        incl = (ss >= tt) if reverse else (ss <= tt)
        ltri = jnp.where((si >= ti) if reverse else (si <= ti), 1.0, 0.0).astype(BF16)
        masks = (same_head, strict, incl, eye, ltri)
        rkv = rkv_ref[...].astype(F32)
        r = rkv[:, 0:d_model_a]
        k = rkv[:, d_model_a:2 * d_model_a]
        v = rkv[:, 2 * d_model_a:3 * d_model_a]
        kk = kk_ref[...].astype(F32)
        ll = ll_ref[...]
        w_raw = w0_ref[d] + _dot(jnp.tanh(ll.astype(F32)).astype(BF16), dup_ref[d])
        logw = -EXP_M05 * _sigmoid(w_raw)
        a = _sigmoid(a0_ref[d] + _dot(ll, aup_ref[d]))
        for hh in range(d_model_a // HALF_W):
            sl = slice(hh * HALF_W, (hh + 1) * HALF_W)
            y_nat, h1 = _rwkv_chunk(r[:, sl], k[:, sl], v[:, sl], kk[:, sl], logw[:, sl], a[:, sl],
                                    k_a[:, sl], h_ref[d, hh], reverse, masks)
            h_ref[d, hh] = h1
            y_ref[:, sl] = y_nat


def _rwkv(p, kk, w0, dup, a0, aup, k_a, bsz, ctx_chunks, lat_chunks):
    m = p.shape[0]
    da = 512
    nch = ctx_chunks + lat_chunks
    n_lowrank_col = (p.shape[1] - 256) // 256

    def bwd_chunk(i):
        return jnp.where(i < ctx_chunks, ctx_chunks - 1 - i, nch - 1 - (i - ctx_chunks))

    fwd = lambda b, i: b * nch + i
    bwd = lambda b, i: b * nch + bwd_chunk(i)
    spec3 = lambda shape: pl.BlockSpec(shape, lambda b, i: (0, 0, 0))
    return pl.pallas_call(
        _rwkv_kernel,
        grid=(bsz, nch),
        in_specs=[pl.BlockSpec((CHUNK, 3 * da), lambda b, i: (fwd(b, i), 0)),
                  pl.BlockSpec((CHUNK, da), lambda b, i: (fwd(b, i), 0)),
                  pl.BlockSpec((CHUNK, 256), lambda b, i: (fwd(b, i), n_lowrank_col)),
                  pl.BlockSpec((CHUNK, 3 * da), lambda b, i: (bwd(b, i), 0)),
                  pl.BlockSpec((CHUNK, da), lambda b, i: (bwd(b, i), 0)),
                  pl.BlockSpec((CHUNK, 256), lambda b, i: (bwd(b, i), n_lowrank_col)),
                  spec3((2, 1, da)), spec3((2, 256, da)), spec3((2, 1, da)), spec3((2, 256, da)),
                  pl.BlockSpec((1, da), lambda b, i: (0, 0))],
        out_specs=[pl.BlockSpec((CHUNK, da), lambda b, i: (fwd(b, i), 0)),
                   pl.BlockSpec((CHUNK, da), lambda b, i: (bwd(b, i), 0))],
        out_shape=[jax.ShapeDtypeStruct((m, da), F32)] * 2,
        scratch_shapes=[pltpu.VMEM((2, da // HALF_W, HALF_W, HALF_W), F32)],
        compiler_params=pltpu.CompilerParams(vmem_limit_bytes=VMEM_LIMIT,
                                             dimension_semantics=("arbitrary", "arbitrary")),
        name="rwkv",
    )(p, kk, p, p, kk, p, w0, dup, a0, aup, k_a)


def _attn_kernel(q_ref, k_ref, v_ref, lq_ref, sg_ref, li_ref, o_ref, *, ctx_tiles, n_ctx):
    lq = lq_ref[...]
    li = li_ref[...]
    lam = (jnp.exp(jnp.sum(lq[0:1] * lq[1:2], axis=-1, keepdims=True))
           - jnp.exp(jnp.sum(lq[2:3] * lq[3:4], axis=-1, keepdims=True)) + li)
    q = q_ref[...]
    lane = lax.broadcasted_iota(jnp.int32, q.shape, 1)
    zero = jnp.zeros_like(q)
    q0 = jnp.where(lane < DH_C, q, zero)
    q1 = jnp.where(lane >= DH_C, q, zero)

    def attend(n_keys):
        k = k_ref[0:n_keys, :]
        v = v_ref[0:n_keys, :]

        def soft(qc):
            s = lax.dot_general(qc, k, NT_DIMS, preferred_element_type=F32)
            e = jnp.exp(s - jnp.max(s, axis=-1, keepdims=True))
            return e, 1.0 / jnp.sum(e, axis=-1, keepdims=True)

        e0, i0 = soft(q0)
        e1, i1 = soft(q1)
        attn = e0 * i0 - e1 * (lam * i1)
        o = _dot(attn.astype(BF16), v)
        ms = jnp.mean(o * o, axis=-1, keepdims=True)
        o_ref[...] = (o * lax.rsqrt(ms + RMS_EPS) * sg_ref[...] * (1.0 - li)).astype(BF16)

    is_ctx = pl.program_id(2) < ctx_tiles

    @pl.when(is_ctx)
    def _():
        attend(n_ctx)

    @pl.when(jnp.logical_not(is_ctx))
    def _():
        attend(k_ref.shape[0])


def _attn(qh, kh, p, lambda_qk, subln_g, lam_init, bsz, n_ctx, n_all):
    m = qh.shape[0]
    n_heads = qh.shape[1] // DV_C
    tiles = n_all // ROW_TILE
    vcol = COL_VD * 512 // DV_C
    kern = functools.partial(_attn_kernel, ctx_tiles=n_ctx // ROW_TILE, n_ctx=n_ctx)
    return pl.pallas_call(
        kern,
        grid=(bsz, n_heads, tiles),
        in_specs=[pl.BlockSpec((ROW_TILE, DV_C), lambda b, h, i: (b * tiles + i, h)),
                  pl.BlockSpec((n_all, DV_C), lambda b, h, i: (b, h)),
                  pl.BlockSpec((n_all, DV_C), lambda b, h, i: (b, vcol + h)),
                  pl.BlockSpec(lambda_qk.shape, lambda b, h, i: (0, 0)),
                  pl.BlockSpec((1, DV_C), lambda b, h, i: (0, 0)),
                  pl.BlockSpec((1, 1), lambda b, h, i: (0, 0))],
        out_specs=pl.BlockSpec((ROW_TILE, DV_C), lambda b, h, i: (b * tiles + i, h)),
        out_shape=jax.ShapeDtypeStruct((m, n_heads * DV_C), BF16),
        compiler_params=pltpu.CompilerParams(vmem_limit_bytes=VMEM_LIMIT),
        name="diff_attn",
    )(qh, kh, p, lambda_qk, subln_g, lam_init)


def _merge_kernel(x_ref, mod_ref, yf_ref, yb_ref, rkv_ref, za_ref, cv_ref, gcp_ref, up_ref, gcn_ref, un_ref,
                  zc_ref, yc_ref, gm_ref, rk_ref, gng_ref, gnb_ref, cw_ref, cb_ref, wbr_ref, wout_ref, ones_ref,
                  o_ref, *, tiles_per_batch, ctx_tiles):
    d = x_ref.shape[1]
    bw = yf_ref.shape[1]
    ones = ones_ref[...]
    t = pl.program_id(0) % tiles_per_batch
    seg_start = jnp.logical_or(t == 0, t == ctx_tiles)
    seg_end = jnp.logical_or(t == ctx_tiles - 1, t == tiles_per_batch - 1)
    silu = lambda z: z * _sigmoid(z)

    y = yf_ref[...] + yb_ref[...]
    mu = _dot_exact_rhs(y, ones) * (1.0 / HEAD_A)
    dev = y - mu
    var = _dot_exact_rhs(dev * dev, ones) * (1.0 / HEAD_A)
    rkv = rkv_ref[...].astype(F32)
    r, k, v = rkv[:, 0:bw], rkv[:, bw:2 * bw], rkv[:, 2 * bw:3 * bw]
    bonus = _dot_exact_rhs(r * k * rk_ref[...], ones) * v
    y_a = dev * lax.rsqrt(var + GN_EPS) * gng_ref[...] + gnb_ref[...] + bonus

    cv = cv_ref[...].astype(F32)
    gb, gc, u, zb = cv[:, 0:bw], cv[:, bw:2 * bw], cv[:, 2 * bw:3 * bw], cv[:, 3 * bw:4 * bw]
    pc = gc * u
    hrows = gcp_ref.shape[0]
    prev_row = gcp_ref[...].astype(F32)[hrows - 1:hrows] * up_ref[...].astype(F32)[hrows - 1:hrows]
    next_row = gcn_ref[...].astype(F32)[0:1] * un_ref[...].astype(F32)[0:1]
    prev_row = jnp.where(seg_start, 0.0, prev_row)
    next_row = jnp.where(seg_end, 0.0, next_row)
    rows = pc.shape[0]
    row = lax.broadcasted_iota(jnp.int32, pc.shape, 0)
    p_prev = jnp.where(row == 0, prev_row, pltpu.roll(pc, 1, 0))
    p_next = jnp.where(row == rows - 1, next_row, pltpu.roll(pc, rows - 1, 0))
    cw = cw_ref[...]
    y_b = gb * (p_prev * cw[0:1] + pc * cw[1:2] + p_next * cw[2:3] + cb_ref[...])

    gm = gm_ref[...].astype(F32)
    branches = ((y_a, za_ref[...].astype(F32)), (y_b, zb), (yc_ref[...].astype(F32), zc_ref[...].astype(F32)))
    acc = jnp.zeros((rows, d), F32)
    for idx, (yy, zz) in enumerate(branches):
        br = _dot((yy * silu(zz)).astype(BF16), wbr_ref[idx])
        acc = acc + _sigmoid(gm[:, idx * d:(idx + 1) * d]) * br
    out = _dot(acc.astype(BF16), wout_ref[...])
    gate = mod_ref[0][:, 2 * d:3 * d]
    o_ref[...] = x_ref[...] + gate * out


def _merge(xs, mod_l, yf, yb, p, yc, r_k, gn_g, gn_b, conv_w, conv_b, w_branch, w_out, ones_bd,
           mod_row, tiles_per_batch, ctx_tiles):
    m, d = xs.shape
    bw = 512
    halo = 16
    per_tile = ROW_TILE // halo
    n_halo = m // halo
    col = lambda c, width=bw: pl.BlockSpec((ROW_TILE, width), lambda i, c=c: (i, c))
    par = lambda shape: pl.BlockSpec(shape, lambda i: tuple(0 for _ in shape))
    prev = lambda c: pl.BlockSpec((halo, bw), lambda i, c=c: (jnp.maximum(i * per_tile - 1, 0), c))
    nxt = lambda c: pl.BlockSpec((halo, bw), lambda i, c=c: (jnp.minimum((i + 1) * per_tile, n_halo - 1), c))
    kern = functools.partial(_merge_kernel, tiles_per_batch=tiles_per_batch, ctx_tiles=ctx_tiles)
    return pl.pallas_call(
        kern,
        grid=(m // ROW_TILE,),
        in_specs=[pl.BlockSpec((ROW_TILE, d), lambda i: (i, 0)),
                  pl.BlockSpec((1, 1, 3 * d), lambda i: (mod_row(i), 0, 0)),
                  col(0), col(0),
                  col(0, 3 * bw), col(COL_ZA), col(1, 4 * bw),
                  prev(COL_GC), prev(COL_U), nxt(COL_GC), nxt(COL_U),
                  col(COL_ZC), col(0), col(2, 3 * d),
                  par((1, bw)), par((1, bw)), par((1, bw)), par(conv_w.shape), par((1, bw)),
                  par(w_branch.shape), par(w_out.shape), par(ones_bd.shape)],
        out_specs=pl.BlockSpec((ROW_TILE, d), lambda i: (i, 0)),
        out_shape=jax.ShapeDtypeStruct((m, d), F32),
        compiler_params=pltpu.CompilerParams(vmem_limit_bytes=VMEM_LIMIT),
        name="merge",
    )(xs, mod_l, yf, yb, p, p, p, p, p, p, p, p, yc, p,
      r_k, gn_g, gn_b, conv_w, conv_b, w_branch, w_out, ones_bd)


def _rope_tables(n_ctx, n_lat, width):
    rows = n_lat // GRID_W
    row = jnp.repeat(jnp.arange(rows), GRID_W).astype(F32)
    col = jnp.tile(jnp.arange(GRID_W), rows).astype(F32)
    n_freq = DH_C // 4
    inv_freq = ROPE_BASE ** (-jnp.arange(n_freq, dtype=F32) / n_freq)
    ar, ac = row[:, None] * inv_freq, col[:, None] * inv_freq
    cos = jnp.concatenate([jnp.cos(ar), jnp.cos(ar), jnp.cos(ac), jnp.cos(ac)], axis=-1)
    sin = jnp.concatenate([-jnp.sin(ar), jnp.sin(ar), -jnp.sin(ac), jnp.sin(ac)], axis=-1)
    cos = jnp.concatenate([jnp.ones((n_ctx, DH_C), F32), cos], axis=0)
    sin = jnp.concatenate([jnp.zeros((n_ctx, DH_C), F32), sin], axis=0)
    reps = width // DH_C
    return jnp.tile(cos, (1, reps)), jnp.tile(sin, (1, reps))


def _permute_w_in(w_in):
    a = 512
    r, k, v = (w_in[..., i * a:(i + 1) * a] for i in range(3))
    low = w_in[..., 3 * a:3 * a + 256]
    rest = w_in[..., 3 * a + 256:]
    return jnp.concatenate([r, k, v, rest[..., :9 * a], rest[..., 9 * a:], low], axis=-1)


def kernel(x, c, ctx, c_ctx, w_mod, b_mod, norm_g, w_in, decay_w0, decay_up, iclr_a0, iclr_up, k_k, k_a, r_k,
           gn_g, gn_b, conv_w, conv_b, qk_norm_g, lambda_qk, subln_g, w_branch, w_out):
    bsz, n_lat, d = x.shape
    n_ctx = ctx.shape[1]
    depth = w_mod.shape[0]
    n_all = n_ctx + n_lat
    assert n_ctx % ROW_TILE == 0 and n_lat % ROW_TILE == 0 and n_lat % GRID_W == 0
    tiles_per_batch = n_all // ROW_TILE
    ctx_tiles = n_ctx // ROW_TILE
    da = decay_w0.shape[-1]
    rank = decay_up.shape[-2]

    n_cond = 16
    assert bsz < n_cond
    cond = jnp.zeros((n_cond, d), F32).at[:bsz].set(c).at[bsz].set(c_ctx)
    mod = _modulation(cond, w_mod, b_mod).reshape(depth, n_cond, 1, 3 * d)

    def mod_row(i):
        return jnp.where(i % tiles_per_batch < ctx_tiles, bsz, i // tiles_per_batch)

    xs = jnp.concatenate([ctx, x], axis=1).reshape(bsz * n_all, d)
    w_in_p = _permute_w_in(w_in).astype(BF16)
    cos_t, sin_t = _rope_tables(n_ctx, n_lat, 512)
    gi = lax.broadcasted_iota(jnp.int32, (512, 512), 0) // HEAD_A
    gj = lax.broadcasted_iota(jnp.int32, (512, 512), 1) // HEAD_A
    ones_bd = (gi == gj).astype(BF16)
    dup = jnp.zeros((depth, 2, 4 * rank, da), F32)
    aup = jnp.zeros((depth, 2, 4 * rank, da), F32)
    for dd in range(2):
        dup = dup.at[:, dd, dd * rank:(dd + 1) * rank].set(decay_up[:, dd])
        aup = aup.at[:, dd, (2 + dd) * rank:(3 + dd) * rank].set(iclr_up[:, dd])
    dup, aup = dup.astype(BF16), aup.astype(BF16)
    w_branch_b, w_out_b = w_branch.astype(BF16), w_out.astype(BF16)
    n_rep = 512 // DH_C

    for l in range(depth):
        lam_init = jnp.full((1, 1), 0.8 - 0.6 * math.exp(-0.3 * l), F32)
        p = _in_proj(xs, mod[l], norm_g[l].reshape(1, d), w_in_p[l], mod_row)
        qh, kh, kk = _prep(p, cos_t, sin_t, jnp.tile(qk_norm_g[l, 0], n_rep).reshape(1, 512),
                           jnp.tile(qk_norm_g[l, 1], n_rep).reshape(1, 512), k_k[l].reshape(1, da), ones_bd,
                           tiles_per_batch)
        yf, yb = _rwkv(p, kk, decay_w0[l].reshape(2, 1, da), dup[l], iclr_a0[l].reshape(2, 1, da), aup[l],
                       k_a[l].reshape(1, da), bsz, n_ctx // CHUNK, n_lat // CHUNK)
        yc = _attn(qh, kh, p, lambda_qk[l], subln_g[l].reshape(1, DV_C), lam_init, bsz, n_ctx, n_all)
        xs = _merge(xs, mod[l], yf, yb, p, yc, r_k[l].reshape(1, da), gn_g[l].reshape(1, da),
                    gn_b[l].reshape(1, da), conv_w[l], conv_b[l].reshape(1, da), w_branch_b[l], w_out_b[l],
                    ones_bd, mod_row, tiles_per_batch, ctx_tiles)
    return xs.reshape(bsz, n_all, d)[:, n_ctx:, :]
```

```python
import functools
import math

import jax
import jax.numpy as jnp
from jax import lax
from jax.experimental import pallas as pl
from jax.experimental.pallas import tpu as pltpu

F32 = jnp.float32
BF16 = jnp.bfloat16

ROW_TILE = 256
CHUNK = 64
RWKV_GROUP = 8
RWKV_STAGGER = 2
BRANCH_W = 512
LOWRANK_W = 256
HALO_ROWS = 16
HEAD_A = 64
HALF_W = 256
HEADS_PER_HALF = HALF_W // HEAD_A
DH_C = 64
DV_C = 128
KEY_BLOCK = 256
ATTN_HEADS = 4
BOUND_MARGIN = 1.02
MAX_SAFE_LOG2 = 40.0
GRID_W = 64
ROPE_BASE = 10000.0
RMS_EPS = 1e-6
GN_EPS = 64e-5
KK_EPS = 1e-12
EXP_M05 = math.exp(-0.5)
LOG2_E = math.log2(math.e)
VMEM_LIMIT = 56 * 1024 * 1024

NT_DIMS = (((1,), (1,)), ((), ()))

COL_R, COL_K, COL_V, COL_ZA, COL_ZB, COL_ZC, COL_GB, COL_PC, COL_BONUS, COL_VD = range(10)
N_GROUPS = 10


def _dot(a, b):
    return jnp.dot(a, b, preferred_element_type=F32)


def _split_bf16(x):
    hi = x.astype(BF16)
    lo = (x - hi.astype(F32)).astype(BF16)
    return hi, lo


def _group_sum(x, ones_half):
    w = ones_half.shape[0]
    parts = [_dot(x[:, i:i + w].astype(BF16), ones_half) for i in range(0, x.shape[1], w)]
    return jnp.concatenate(parts, axis=1)


def _dot_exact_lhs(w_bf16, x):
    hi, lo = _split_bf16(x)
    return _dot(w_bf16, hi) + _dot(w_bf16, lo)


def _dot3(a, b):
    ah, al = _split_bf16(a)
    bh, bl = _split_bf16(b)
    return _dot(ah, bh) + _dot(ah, bl) + _dot(al, bh)


def _sigmoid(x):
    return 1.0 / (1.0 + jnp.exp(-x))


def _mod_kernel(cond_ref, w_ref, b_ref, o_ref):
    cnd = cond_ref[...]
    o_ref[0] = _dot3(cnd * _sigmoid(cnd), w_ref[0]) + b_ref[0]


def _modulation(cond, w_mod, b_mod):
    depth, d, d3 = w_mod.shape
    n = cond.shape[0]
    return pl.pallas_call(
        _mod_kernel,
        grid=(depth, d3 // d),
        in_specs=[pl.BlockSpec((n, d), lambda l, j: (0, 0)),
                  pl.BlockSpec((1, d, d), lambda l, j: (l, 0, j)),
                  pl.BlockSpec((1, 1, d), lambda l, j: (l, 0, j))],
        out_specs=pl.BlockSpec((1, n, d), lambda l, j: (l, 0, j)),
        out_shape=jax.ShapeDtypeStruct((depth, n, d3), F32),
        name="modulation",
    )(cond, w_mod, b_mod.reshape(depth, 1, d3))


def _in_kernel(xc_ref, xl_ref, mod_ref, g_ref, w_ref, cos_ref, sin_ref, gq_ref, gk_ref, kkw_ref, rk_ref, ones_ref,
               p_ref, qo_ref, ko_ref, kko_ref, *, tiles_per_batch, ctx_tiles):
    d = xc_ref.shape[2]
    is_ctx = pl.program_id(0) % tiles_per_batch < ctx_tiles
    x = jnp.where(is_ctx, xc_ref[0], xl_ref[0])
    ms = jnp.mean(x * x, axis=-1, keepdims=True)
    y = x * lax.rsqrt(ms + RMS_EPS) * g_ref[...]
    m = mod_ref[0]
    h = y * (1.0 + m[:, d:2 * d]) + m[:, 0:d]
    hb = h.astype(BF16)
    w = qo_ref.shape[1]
    n_head = 5 * w
    head = _dot(hb, w_ref[:, 0:n_head])
    p_ref[:, 0:3 * w] = head[:, 2 * w:].astype(BF16)

    ones = ones_ref[...]
    cos = cos_ref[...]
    sin = sin_ref[...]
    lane = lax.broadcasted_iota(jnp.int32, qo_ref.shape, 1)
    first = (lane & 31) < 16

    def norm_rope(t, g):
        msq = _group_sum(t * t, ones) * (1.0 / DH_C)
        tn = t * lax.rsqrt(msq + RMS_EPS) * g
        sw = jnp.where(first, pltpu.roll(tn, w - 16, 1), pltpu.roll(tn, 16, 1))
        return tn * cos + sw * sin

    def q_out():
        qo_ref[...] = (norm_rope(head[:, 0:w], gq_ref[...]) * (LOG2_E * DH_C ** -0.5)).astype(BF16)

    def k_out():
        ko_ref[...] = norm_rope(head[:, w:2 * w], gk_ref[...]).astype(BF16)

    def kk_out():
        kr = head[:, (2 + COL_K) * w:(3 + COL_K) * w] * kkw_ref[...]
        kko_ref[...] = (kr * lax.rsqrt(_group_sum(kr * kr, ones) + KK_EPS)).astype(BF16)

    def conv_out():
        res = _dot(hb, w_ref[:, (2 + COL_GB) * w:(2 + N_GROUPS) * w])
        r, k, v = (head[:, (2 + c) * w:(3 + c) * w] for c in (COL_R, COL_K, COL_V))
        p_ref[:, COL_GB * w:(COL_GB + 1) * w] = res[:, 0:w].astype(BF16)
        p_ref[:, COL_PC * w:(COL_PC + 1) * w] = (res[:, w:2 * w] * res[:, 2 * w:3 * w]).astype(BF16)
        p_ref[:, COL_BONUS * w:(COL_BONUS + 1) * w] = (_group_sum(r * k * rk_ref[...], ones) * v).astype(BF16)
        p_ref[:, COL_VD * w:(COL_VD + 1) * w] = res[:, 3 * w:4 * w].astype(BF16)

    def act_out(lo, hi, act):
        p_ref[:, lo:hi] = act(_dot(hb, w_ref[:, 2 * w + lo:2 * w + hi])).astype(BF16)

    silu = lambda z: z * _sigmoid(z)
    n_cols = p_ref.shape[1]
    blocks = ([functools.partial(act_out, COL_ZA * w, COL_GB * w, silu), conv_out]
              + [functools.partial(act_out, lo, lo + 2 * w, _sigmoid)
                 for lo in range(N_GROUPS * w, N_GROUPS * w + 3 * d, 2 * w)]
              + [functools.partial(act_out, N_GROUPS * w + 3 * d, n_cols, lambda z: z)])
    epilogue = [q_out, k_out, kk_out]
    for idx, block in enumerate(blocks):
        block()
        if idx < len(epilogue):
            epilogue[idx]()


def _stream_specs(d, tiles_per_batch, ctx_tiles):
    tile = lambda i: i % tiles_per_batch
    ctx = pl.BlockSpec((1, ROW_TILE, d), lambda i: (i // tiles_per_batch, jnp.minimum(tile(i), ctx_tiles - 1), 0))
    lat = pl.BlockSpec((1, ROW_TILE, d), lambda i: (i // tiles_per_batch, jnp.maximum(tile(i) - ctx_tiles, 0), 0))
    return ctx, lat


def _in_proj(xc, xl, mod_l, norm_g, w_in, cos_t, sin_t, gq, gk, k_k, r_k, ones_bd, mod_row, tiles_per_batch,
             ctx_tiles):
    bsz, _, d = xc.shape
    m = bsz * tiles_per_batch * ROW_TILE
    n = w_in.shape[1]
    w = BRANCH_W
    n_keep = n - 2 * w
    par = lambda shape: pl.BlockSpec(shape, lambda i: tuple(0 for _ in shape))
    tab = pl.BlockSpec((ROW_TILE, w), lambda i: (i % tiles_per_batch, 0))
    out = pl.BlockSpec((ROW_TILE, w), lambda i: (i, 0))
    kern = functools.partial(_in_kernel, tiles_per_batch=tiles_per_batch, ctx_tiles=ctx_tiles)
    return pl.pallas_call(
        kern,
        grid=(m // ROW_TILE,),
        in_specs=[*_stream_specs(d, tiles_per_batch, ctx_tiles),
                  pl.BlockSpec((1, 1, 3 * d), lambda i: (mod_row(i), 0, 0)),
                  par((1, d)),
                  pl.BlockSpec((d, n), lambda i: (0, 0), pipeline_mode=pl.Buffered(1)),
                  tab, tab, par((1, w)), par((1, w)), par((1, w)), par((1, w)), par(ones_bd.shape)],
        out_specs=[pl.BlockSpec((ROW_TILE, n_keep), lambda i: (i, 0)), out, out, out],
        out_shape=[jax.ShapeDtypeStruct((m, n_keep), BF16)] + [jax.ShapeDtypeStruct((m, w), BF16)] * 3,
        compiler_params=pltpu.CompilerParams(vmem_limit_bytes=VMEM_LIMIT),
        name="in_proj",
    )(xc, xl, mod_l, norm_g, w_in, cos_t, sin_t, gq, gk, k_k, r_k, ones_bd)


def _expand(z, lane_masks):
    zb = z.astype(BF16)
    zeros = jnp.zeros((z.shape[0], 2 * HEAD_A), BF16)
    blocks = []
    for hd in range(HEADS_PER_HALF):
        pair = hd // 2
        piece = zb[:, pair * 2 * HEAD_A:(pair + 1) * 2 * HEAD_A] * lane_masks[hd % 2]
        blocks.append(jnp.concatenate([piece if i == pair else zeros for i in range(HEADS_PER_HALF // 2)], axis=1))
    return jnp.concatenate(blocks, axis=0)


def _head_transpose(z):
    zt = z.T
    return jnp.concatenate([zt[hd * HEAD_A:(hd + 1) * HEAD_A] for hd in range(HEADS_PER_HALF)], axis=1)


def _rwkv_chunks(chains, lane_masks, eye, dir_masks):
    bf = lambda x: x.astype(BF16)
    ex = lambda z: _expand(z, lane_masks)
    stack = lambda *xs: jnp.concatenate([bf(x) for x in xs], axis=0)

    def prep(c):
        strict, incl, ltri = dir_masks[c["reverse"]]
        cum = _dot_exact_lhs(ltri, c["logw"])
        cum_end = cum[0:1] if c["reverse"] else cum[CHUNK - 1:CHUNK]
        kdir = c["k"] * (1.0 + (c["a"] - 1.0) * c["k_a"])
        akk = c["a"] * c["kk"]
        en = jnp.exp(-cum)
        ec = jnp.exp(cum_end - cum)
        c.update(strict=strict, incl=incl, kt=c["kk"] * jnp.exp(cum - c["logw"]), rt=c["r"] * jnp.exp(cum),
                 p_end=jnp.exp(cum_end), an_t=_head_transpose(akk * en), kn_t=_head_transpose(kdir * en),
                 ab_t=_head_transpose(akk * ec), kb_t=_head_transpose(kdir * ec))

    def gram(c):
        lhs = stack(c["kt"], c["rt"])
        g_a = _dot(lhs, ex(c["an_t"]))
        g_k = _dot(lhs, ex(c["kn_t"]))
        n = jnp.where(c["strict"], g_a[0:CHUNK], 0.0)
        c.update(n=n, t=jnp.where(eye, 1.0, 0.0) - n, b_ra=jnp.where(c["incl"], g_a[CHUNK:], 0.0),
                 a_kk=jnp.where(c["strict"], g_k[0:CHUNK], 0.0), b_rk=jnp.where(c["incl"], g_k[CHUNK:], 0.0))

    def square(c):
        c["npow"] = _dot(bf(c["n"]), ex(c["n"]))

    def inverse_step(c, last):
        w = ex(c["npow"])
        if last:
            c["t"] = c["t"] + _dot(bf(c["t"]), w)
        else:
            x = _dot(stack(c["t"], c["npow"]), w)
            c["t"] = c["t"] + x[0:CHUNK]
            c["npow"] = x[CHUNK:]

    def against_v(c):
        x = _dot(stack(c["a_kk"], c["b_rk"], c["kb_t"]), ex(c["v"]))
        c.update(w1=x[0:CHUNK], brk_v=x[CHUNK:2 * CHUNK], kbt_v=x[2 * CHUNK:])

    def apply_inverse(c):
        tb = bf(c["t"])
        c.update(khat=_dot(tb, ex(c["kt"])), uhat=_dot(tb, ex(c["w1"])))

    def chunk_maps(c):
        lhs = stack(c["b_ra"], c["ab_t"])
        xk = _dot(lhs, ex(c["khat"]))
        xu = _dot(lhs, ex(c["uhat"]))
        c.update(rhat=c["rt"] - xk[0:CHUNK], m=jnp.where(eye, c["p_end"], 0.0) - xk[CHUNK:],
                 yhat=c["brk_v"] - xu[0:CHUNK], g=c["kbt_v"] - xu[CHUNK:])

    def advance(c):
        x = _dot(stack(c["m"], c["rhat"]), ex(c["h0"]))
        c.update(y=x[CHUNK:] + c["yhat"], h1=x[0:CHUNK] + c["g"])

    stages = ([prep, gram, square] + [functools.partial(inverse_step, last=i == 4) for i in range(5)]
              + [against_v, apply_inverse, chunk_maps, advance])
    n_groups = min(RWKV_STAGGER, len(chains))
    groups = [chains[i::n_groups] for i in range(n_groups)]
    for tick in range(len(stages) + n_groups - 1):
        for gi, grp in enumerate(groups):
            if 0 <= tick - gi < len(stages):
                for c in grp:
                    stages[tick - gi](c)


def _rwkv_kernel(rkvf_ref, kkf_ref, llf_ref, rkvb_ref, kkb_ref, llb_ref,
                 w0_ref, dup_ref, a0_ref, aup_ref, ka_ref, yf_ref, yb_ref, h_ref):
    @pl.when(pl.program_id(1) == 0)
    def _():
        h_ref[...] = jnp.zeros(h_ref.shape, F32)

    first = lax.broadcasted_iota(jnp.int32, (CHUNK, 2 * HEAD_A), 1) < HEAD_A
    lane_masks = (jnp.where(first, 1.0, 0.0).astype(BF16), jnp.where(first, 0.0, 1.0).astype(BF16))
    tt = lax.broadcasted_iota(jnp.int32, (CHUNK, HALF_W), 0)
    ss = lax.broadcasted_iota(jnp.int32, (CHUNK, HALF_W), 1) & (CHUNK - 1)
    eye = tt == ss
    ti = lax.broadcasted_iota(jnp.int32, (CHUNK, CHUNK), 0)
    si = lax.broadcasted_iota(jnp.int32, (CHUNK, CHUNK), 1)
    dir_masks = {False: (ss < tt, ss <= tt, jnp.where(si <= ti, 1.0, 0.0).astype(BF16)),
                 True: (ss > tt, ss >= tt, jnp.where(si >= ti, 1.0, 0.0).astype(BF16))}
    group = rkvf_ref.shape[0]
    d_model_a = rkvf_ref.shape[2] // 3
    k_a = ka_ref[...]

    chains = []
    for d, (rkv_ref, kk_ref, ll_ref, y_ref) in enumerate(
            ((rkvf_ref, kkf_ref, llf_ref, yf_ref), (rkvb_ref, kkb_ref, llb_ref, yb_ref))):
        ll = ll_ref[...].reshape(group * CHUNK, ll_ref.shape[2])
        w_raw = w0_ref[d] + _dot(jnp.tanh(ll.astype(F32)).astype(BF16), dup_ref[d])
        logw_all = -EXP_M05 * _sigmoid(w_raw)
        a_all = _sigmoid(a0_ref[d] + _dot(ll, aup_ref[d]))
        for g in range(group):
            rows = slice(g * CHUNK, (g + 1) * CHUNK)
            rkv = rkv_ref[g].astype(F32)
            kk = kk_ref[g].astype(F32)
            for hh in range(d_model_a // HALF_W):
                sl = slice(hh * HALF_W, (hh + 1) * HALF_W)
                chains.append(dict(
                    r=rkv[:, hh * HALF_W:(hh + 1) * HALF_W],
                    k=rkv[:, d_model_a + hh * HALF_W:d_model_a + (hh + 1) * HALF_W],
                    v=rkv[:, 2 * d_model_a + hh * HALF_W:2 * d_model_a + (hh + 1) * HALF_W],
                    kk=kk[:, sl], logw=logw_all[rows, sl], a=a_all[rows, sl], k_a=k_a[:, sl],
                    h0=h_ref[g, d, :, sl], reverse=d == 1, out=(y_ref, g, d, sl)))
    _rwkv_chunks(chains, lane_masks, eye, dir_masks)
    for c in chains:
        y_ref, g, d, sl = c["out"]
        h_ref[g, d, :, sl] = c["h1"]
        y_ref[g, :, sl] = c["y"].astype(y_ref.dtype)


def _rwkv(p, kk, w0, dup, a0, aup, k_a, bsz, ctx_chunks, lat_chunks):
    n_all = p.shape[1]
    da = BRANCH_W
    nch = ctx_chunks + lat_chunks
    n_lowrank_col = (p.shape[2] - LOWRANK_W) // LOWRANK_W
    group = math.gcd(bsz, RWKV_GROUP)

    def bwd(i):
        return jnp.where(i < ctx_chunks, ctx_chunks - 1 - i, nch - 1 - (i - ctx_chunks))

    spec3 = lambda shape: pl.BlockSpec(shape, lambda b, i: (0, 0, 0))
    return pl.pallas_call(
        _rwkv_kernel,
        grid=(bsz // group, nch),
        in_specs=[pl.BlockSpec((group, CHUNK, 3 * da), lambda b, i: (b, i, 0)),
                  pl.BlockSpec((group, CHUNK, da), lambda b, i: (b, i, 0)),
                  pl.BlockSpec((group, CHUNK, LOWRANK_W), lambda b, i: (b, i, n_lowrank_col)),
                  pl.BlockSpec((group, CHUNK, 3 * da), lambda b, i: (b, bwd(i), 0)),
                  pl.BlockSpec((group, CHUNK, da), lambda b, i: (b, bwd(i), 0)),
                  pl.BlockSpec((group, CHUNK, LOWRANK_W), lambda b, i: (b, bwd(i), n_lowrank_col)),
                  spec3((2, 1, da)), spec3((2, LOWRANK_W, da)), spec3((2, 1, da)), spec3((2, LOWRANK_W, da)),
                  pl.BlockSpec((1, da), lambda b, i: (0, 0))],
        out_specs=[pl.BlockSpec((group, CHUNK, da), lambda b, i: (b, i, 0)),
                   pl.BlockSpec((group, CHUNK, da), lambda b, i: (b, bwd(i), 0))],
        out_shape=[jax.ShapeDtypeStruct((bsz, n_all, da), BF16)] * 2,
        scratch_shapes=[pltpu.VMEM((group, 2, CHUNK, da), F32)],
        compiler_params=pltpu.CompilerParams(vmem_limit_bytes=VMEM_LIMIT,
                                             dimension_semantics=("arbitrary", "arbitrary")),
        name="rwkv",
    )(p, kk, p, p, kk, p, w0, dup, a0, aup, k_a)


def _attn_kernel(q_ref, k_ref, v_ref, g_ref, lq_ref, sg_ref, li_ref, o_ref, p_scr, *, ctx_tiles, n_ctx):
    lq = lq_ref[...]
    li = li_ref[...]
    lam = (jnp.exp(jnp.sum(lq[0:1] * lq[1:2], axis=-1, keepdims=True))
           - jnp.exp(jnp.sum(lq[2:3] * lq[3:4], axis=-1, keepdims=True)) + li)
    n_heads = q_ref.shape[1] // DV_C
    n_all = k_ref.shape[0]
    lane = lax.broadcasted_iota(jnp.int32, (q_ref.shape[0], DV_C), 1)

    g = jnp.abs(g_ref[...])
    bound = (BOUND_MARGIN * LOG2_E * DH_C ** 0.5) * (jnp.max(g[0:1], axis=-1, keepdims=True)
                                                     * jnp.max(g[1:2], axis=-1, keepdims=True))

    def head_queries(h):
        q = q_ref[:, h * DV_C:(h + 1) * DV_C]
        zero = jnp.zeros_like(q)
        return jnp.where(lane < DH_C, q, zero), jnp.where(lane >= DH_C, q, zero)

    def finish(h, o):
        ms = jnp.mean(o * o, axis=-1, keepdims=True)
        o_ref[:, h * DV_C:(h + 1) * DV_C] = (o * lax.rsqrt(ms + RMS_EPS) * sg_ref[...] * (1.0 - li)).astype(BF16)

    def attend_bounded(h, n_keys):
        cols = slice(h * DV_C, (h + 1) * DV_C)
        qs = head_queries(h)
        half = KEY_BLOCK // 2
        sums = []
        for c in range(2):
            run = None
            for j in range(n_keys // KEY_BLOCK):
                blk = slice(j * KEY_BLOCK, (j + 1) * KEY_BLOCK)
                s = lax.dot_general(qs[c], k_ref[blk, cols], NT_DIMS, preferred_element_type=F32)
                e = jnp.exp2(s)
                p_scr[h, c, :, blk] = e.astype(BF16)
                part = e[:, 0:half] + e[:, half:]
                run = part if run is None else run + part
            sums.append(jnp.sum(run, axis=-1, keepdims=True))
        l0, l1 = sums
        attn = p_scr[h, 0, :, 0:n_keys] - p_scr[h, 1, :, 0:n_keys] * (lam * l0 / l1).astype(BF16)
        finish(h, _dot(attn, v_ref[0:n_keys, cols]) * (1.0 / l0))

    def attend_exact(h, n_keys):
        cols = slice(h * DV_C, (h + 1) * DV_C)
        qs = head_queries(h)
        k = k_ref[0:n_keys, cols]

        def soft(qc):
            s = lax.dot_general(qc, k, NT_DIMS, preferred_element_type=F32)
            e = jnp.exp2(s - jnp.max(s, axis=-1, keepdims=True))
            return e, 1.0 / jnp.sum(e, axis=-1, keepdims=True)

        e0, i0 = soft(qs[0])
        e1, i1 = soft(qs[1])
        attn = e0 * i0 - e1 * (lam * i1)
        finish(h, _dot(attn.astype(BF16), v_ref[0:n_keys, cols]))

    is_ctx = pl.program_id(2) < ctx_tiles
    bounded = jnp.max(bound) < MAX_SAFE_LOG2
    for n_keys, tile_pred in ((n_ctx, is_ctx), (n_all, jnp.logical_not(is_ctx))):
        @pl.when(jnp.logical_and(tile_pred, bounded))
        def _(n_keys=n_keys):
            for h in range(n_heads):
                attend_bounded(h, n_keys)

        @pl.when(jnp.logical_and(tile_pred, jnp.logical_not(bounded)))
        def _(n_keys=n_keys):
            for h in range(n_heads):
                attend_exact(h, n_keys)


def _attn(qh, kh, p, qk_g, lambda_qk, subln_g, lam_init, bsz, n_ctx, n_all):
    m = qh.shape[0]
    width = ATTN_HEADS * DV_C
    n_groups = qh.shape[1] // width
    tiles = n_all // ROW_TILE
    vcol = COL_VD * BRANCH_W // width
    kern = functools.partial(_attn_kernel, ctx_tiles=n_ctx // ROW_TILE, n_ctx=n_ctx)
    return pl.pallas_call(
        kern,
        grid=(bsz, n_groups, tiles),
        in_specs=[pl.BlockSpec((ROW_TILE, width), lambda b, h, i: (b * tiles + i, h)),
                  pl.BlockSpec((n_all, width), lambda b, h, i: (b, h)),
                  pl.BlockSpec((n_all, width), lambda b, h, i: (b, vcol + h)),
                  pl.BlockSpec(qk_g.shape, lambda b, h, i: (0, 0)),
                  pl.BlockSpec(lambda_qk.shape, lambda b, h, i: (0, 0)),
                  pl.BlockSpec((1, DV_C), lambda b, h, i: (0, 0)),
                  pl.BlockSpec((1, 1), lambda b, h, i: (0, 0))],
        out_specs=pl.BlockSpec((ROW_TILE, width), lambda b, h, i: (b * tiles + i, h)),
        out_shape=jax.ShapeDtypeStruct((m, qh.shape[1]), BF16),
        scratch_shapes=[pltpu.VMEM((ATTN_HEADS, 2, ROW_TILE, n_all), BF16)],
        compiler_params=pltpu.CompilerParams(vmem_limit_bytes=VMEM_LIMIT),
        name="diff_attn",
    )(qh, kh, p, qk_g, lambda_qk, subln_g, lam_init)


def _merge_kernel(xc_ref, xl_ref, mod_ref, yf_ref, yb_ref, cv_ref, pcp_ref, pcn_ref,
                  zs_ref, yc_ref, gm0_ref, gm1_ref, gm2_ref, gng_ref, gnb_ref, cw_ref, cb_ref, wbr_ref,
                  wout_ref, ones_ref, oc_ref, ol_ref, *, tiles_per_batch, ctx_tiles):
    d = xc_ref.shape[2]
    bw = yf_ref.shape[1]
    ones = ones_ref[...]
    t = pl.program_id(0) % tiles_per_batch
    seg_start = jnp.logical_or(t == 0, t == ctx_tiles)
    seg_end = jnp.logical_or(t == ctx_tiles - 1, t == tiles_per_batch - 1)

    y = yf_ref[...].astype(F32) + yb_ref[...].astype(F32)
    mu = _group_sum(y, ones) * (1.0 / HEAD_A)
    dev = y - mu
    var = _group_sum(dev * dev, ones) * (1.0 / HEAD_A)
    cv = cv_ref[...].astype(F32)
    gb, pc, bonus = cv[:, 0:bw], cv[:, bw:2 * bw], cv[:, 2 * bw:3 * bw]
    y_a = dev * lax.rsqrt(var + GN_EPS) * gng_ref[...] + gnb_ref[...] + bonus

    hrows = pcp_ref.shape[0]
    prev_row = pcp_ref[...].astype(F32)[hrows - 1:hrows]
    next_row = pcn_ref[...].astype(F32)[0:1]
    prev_row = jnp.where(seg_start, 0.0, prev_row)
    next_row = jnp.where(seg_end, 0.0, next_row)
    rows = pc.shape[0]
    row = lax.broadcasted_iota(jnp.int32, pc.shape, 0)
    p_prev = jnp.where(row == 0, prev_row, pltpu.roll(pc, 1, 0))
    p_next = jnp.where(row == rows - 1, next_row, pltpu.roll(pc, rows - 1, 0))
    cw = cw_ref[...]
    y_b = gb * (p_prev * cw[0:1] + pc * cw[1:2] + p_next * cw[2:3] + cb_ref[...])

    zs = zs_ref[...].astype(F32)
    acc = jnp.zeros((rows, d), F32)
    for idx, (yy, gm_ref) in enumerate(zip((y_a, y_b, yc_ref[...].astype(F32)), (gm0_ref, gm1_ref, gm2_ref))):
        br = _dot((yy * zs[:, idx * bw:(idx + 1) * bw]).astype(BF16), wbr_ref[idx])
        acc = acc + gm_ref[...].astype(F32) * br
    out = _dot(acc.astype(BF16), wout_ref[...])
    gate = mod_ref[0][:, 2 * d:3 * d]
    is_ctx = t < ctx_tiles
    res = jnp.where(is_ctx, xc_ref[0], xl_ref[0]) + gate * out

    @pl.when(is_ctx)
    def _():
        oc_ref[0] = res

    @pl.when(jnp.logical_not(is_ctx))
    def _():
        ol_ref[0] = res


def _merge(xc, xl, mod_l, yf, yb, p, yc, gn_g, gn_b, conv_w, conv_b, w_branch, w_out, ones_bd,
           mod_row, tiles_per_batch, ctx_tiles):
    bsz, _, d = xc.shape
    m = bsz * tiles_per_batch * ROW_TILE
    bw = BRANCH_W
    halo = HALO_ROWS
    per_tile = ROW_TILE // halo
    n_halo = m // halo
    col = lambda c, width=bw: pl.BlockSpec((ROW_TILE, width), lambda i, c=c: (i, c))
    par = lambda shape: pl.BlockSpec(shape, lambda i: tuple(0 for _ in shape))
    prev = lambda c: pl.BlockSpec((halo, bw), lambda i, c=c: (jnp.maximum(i * per_tile - 1, 0), c))
    nxt = lambda c: pl.BlockSpec((halo, bw), lambda i, c=c: (jnp.minimum((i + 1) * per_tile, n_halo - 1), c))
    gmcol = lambda c: pl.BlockSpec((ROW_TILE, d), lambda i, c=c: (i, N_GROUPS * bw // d + c))
    kern = functools.partial(_merge_kernel, tiles_per_batch=tiles_per_batch, ctx_tiles=ctx_tiles)
    return pl.pallas_call(
        kern,
        grid=(m // ROW_TILE,),
        in_specs=[*_stream_specs(d, tiles_per_batch, ctx_tiles),
                  pl.BlockSpec((1, 1, 3 * d), lambda i: (mod_row(i), 0, 0)),
                  col(0), col(0),
                  col(COL_GB // 3, 3 * bw), prev(COL_PC), nxt(COL_PC),
                  col(COL_ZA // 3, 3 * bw), col(0), gmcol(0), gmcol(1), gmcol(2),
                  par((1, bw)), par((1, bw)), par(conv_w.shape), par((1, bw)),
                  par(w_branch.shape), par(w_out.shape), par(ones_bd.shape)],
        out_specs=list(_stream_specs(d, tiles_per_batch, ctx_tiles)),
        out_shape=[jax.ShapeDtypeStruct(xc.shape, F32), jax.ShapeDtypeStruct(xl.shape, F32)],
        compiler_params=pltpu.CompilerParams(vmem_limit_bytes=VMEM_LIMIT),
        name="merge",
    )(xc, xl, mod_l, yf, yb, p, p, p, p, yc, p, p, p,
      gn_g, gn_b, conv_w, conv_b, w_branch, w_out, ones_bd)


def _rope_tables(n_ctx, n_lat, width):
    rows = n_lat // GRID_W
    row = jnp.repeat(jnp.arange(rows), GRID_W).astype(F32)
    col = jnp.tile(jnp.arange(GRID_W), rows).astype(F32)
    n_freq = DH_C // 4
    inv_freq = ROPE_BASE ** (-jnp.arange(n_freq, dtype=F32) / n_freq)
    ar, ac = row[:, None] * inv_freq, col[:, None] * inv_freq
    cos = jnp.concatenate([jnp.cos(ar), jnp.cos(ar), jnp.cos(ac), jnp.cos(ac)], axis=-1)
    sin = jnp.concatenate([-jnp.sin(ar), jnp.sin(ar), -jnp.sin(ac), jnp.sin(ac)], axis=-1)
    cos = jnp.concatenate([jnp.ones((n_ctx, DH_C), F32), cos], axis=0)
    sin = jnp.concatenate([jnp.zeros((n_ctx, DH_C), F32), sin], axis=0)
    reps = width // DH_C
    return jnp.tile(cos, (1, reps)), jnp.tile(sin, (1, reps))


def _permute_w_in(w_in):
    a = BRANCH_W
    r, k, v = (w_in[..., i * a:(i + 1) * a] for i in range(3))
    low = w_in[..., 3 * a:3 * a + LOWRANK_W]
    rest = w_in[..., 3 * a + LOWRANK_W:]
    za, gb_gc_u, zb = rest[..., :a], rest[..., a:4 * a], rest[..., 4 * a:5 * a]
    qd_kd, vd, zc, gm = rest[..., 5 * a:7 * a], rest[..., 7 * a:8 * a], rest[..., 8 * a:9 * a], rest[..., 9 * a:]
    return jnp.concatenate([qd_kd, r, k, v, za, zb, zc, gb_gc_u, vd, gm, low], axis=-1)


def kernel(x, c, ctx, c_ctx, w_mod, b_mod, norm_g, w_in, decay_w0, decay_up, iclr_a0, iclr_up, k_k, k_a, r_k,
           gn_g, gn_b, conv_w, conv_b, qk_norm_g, lambda_qk, subln_g, w_branch, w_out):
    bsz, n_lat, d = x.shape
    n_ctx = ctx.shape[1]
    depth = w_mod.shape[0]
    n_all = n_ctx + n_lat
    assert n_ctx % ROW_TILE == 0 and n_lat % ROW_TILE == 0 and n_lat % GRID_W == 0
    tiles_per_batch = n_all // ROW_TILE
    ctx_tiles = n_ctx // ROW_TILE
    da = decay_w0.shape[-1]
    rank = decay_up.shape[-2]

    n_cond = 16
    assert bsz < n_cond
    cond = jnp.zeros((n_cond, d), F32).at[:bsz].set(c).at[bsz].set(c_ctx)
    mod = _modulation(cond, w_mod, b_mod).reshape(depth, n_cond, 1, 3 * d)

    def mod_row(i):
        return jnp.where(i % tiles_per_batch < ctx_tiles, bsz, i // tiles_per_batch)

    xc, xl = ctx, x
    w_in_p = _permute_w_in(w_in.astype(BF16))
    cos_t, sin_t = _rope_tables(n_ctx, n_lat, BRANCH_W)
    gi = lax.broadcasted_iota(jnp.int32, (HALF_W, HALF_W), 0) // HEAD_A
    gj = lax.broadcasted_iota(jnp.int32, (HALF_W, HALF_W), 1) // HEAD_A
    ones_bd = (gi == gj).astype(BF16)
    dup = jnp.zeros((depth, 2, 4 * rank, da), F32)
    aup = jnp.zeros((depth, 2, 4 * rank, da), F32)
    for dd in range(2):
        dup = dup.at[:, dd, dd * rank:(dd + 1) * rank].set(decay_up[:, dd])
        aup = aup.at[:, dd, (2 + dd) * rank:(3 + dd) * rank].set(iclr_up[:, dd])
    dup, aup = dup.astype(BF16), aup.astype(BF16)
    w_branch_b, w_out_b = w_branch.astype(BF16), w_out.astype(BF16)
    n_rep = BRANCH_W // DH_C

    for l in range(depth):
        lam_init = jnp.full((1, 1), 0.8 - 0.6 * math.exp(-0.3 * l), F32)
        p, qh, kh, kk = _in_proj(xc, xl, mod[l], norm_g[l].reshape(1, d), w_in_p[l], cos_t, sin_t,
                                 jnp.tile(qk_norm_g[l, 0], n_rep).reshape(1, BRANCH_W),
                                 jnp.tile(qk_norm_g[l, 1], n_rep).reshape(1, BRANCH_W), k_k[l].reshape(1, da),
                                 r_k[l].reshape(1, da), ones_bd, mod_row, tiles_per_batch, ctx_tiles)
        yf, yb = _rwkv(p.reshape(bsz, n_all, -1), kk.reshape(bsz, n_all, da), decay_w0[l].reshape(2, 1, da), dup[l],
                       iclr_a0[l].reshape(2, 1, da), aup[l], k_a[l].reshape(1, da), bsz, n_ctx // CHUNK,
                       n_lat // CHUNK)
        yf, yb = yf.reshape(bsz * n_all, da), yb.reshape(bsz * n_all, da)
        yc = _attn(qh, kh, p, qk_norm_g[l], lambda_qk[l], subln_g[l].reshape(1, DV_C), lam_init, bsz, n_ctx, n_all)
        xc, xl = _merge(xc, xl, mod[l], yf, yb, p, yc, gn_g[l].reshape(1, da),
                        gn_b[l].reshape(1, da), conv_w[l], conv_b[l].reshape(1, da), w_branch_b[l], w_out_b[l],
                        ones_bd, mod_row, tiles_per_batch, ctx_tiles)
    return xl
```

```python
import functools
import math

import jax
import jax.numpy as jnp
from jax import lax
from jax.experimental import pallas as pl
from jax.experimental.pallas import tpu as pltpu

F32 = jnp.float32
BF16 = jnp.bfloat16

ROW_TILE = 256
CHUNK = 64
RWKV_GROUP = 8
RWKV_STAGGER = 2
BRANCH_W = 512
LOWRANK_W = 256
HALO_ROWS = 16
HEAD_A = 64
HALF_W = 256
HEADS_PER_HALF = HALF_W // HEAD_A
DH_C = 64
DV_C = 128
KEY_BLOCK = 256
ATTN_HEADS = 4
VALUE_CHUNKS = 2
BOUND_MARGIN = 1.02
MAX_SAFE_LOG2 = 40.0
GRID_W = 64
ROPE_BASE = 10000.0
RMS_EPS = 1e-6
GN_EPS = 64e-5
KK_EPS = 1e-12
EXP_M05 = math.exp(-0.5)
LOG2_E = math.log2(math.e)
VMEM_LIMIT = 56 * 1024 * 1024

NT_DIMS = (((1,), (1,)), ((), ()))

COL_R, COL_K, COL_V, COL_ZA, COL_ZB, COL_ZC, COL_GB, COL_PC, COL_BONUS, COL_VD = range(10)
N_GROUPS = 10


def _dot(a, b):
    return jnp.dot(a, b, preferred_element_type=F32)


def _split_bf16(x):
    hi = x.astype(BF16)
    lo = (x - hi.astype(F32)).astype(BF16)
    return hi, lo


def _group_sum(x, ones_half):
    w = ones_half.shape[0]
    parts = [_dot(x[:, i:i + w].astype(BF16), ones_half) for i in range(0, x.shape[1], w)]
    return jnp.concatenate(parts, axis=1)


def _dot_exact_lhs(w_bf16, x):
    hi, lo = _split_bf16(x)
    return _dot(w_bf16, hi) + _dot(w_bf16, lo)


def _dot3(a, b):
    ah, al = _split_bf16(a)
    bh, bl = _split_bf16(b)
    return _dot(ah, bh) + _dot(ah, bl) + _dot(al, bh)


def _sigmoid(x):
    return 1.0 / (1.0 + jnp.exp(-x))


def _mod_kernel(cond_ref, w_ref, b_ref, o_ref):
    cnd = cond_ref[...]
    o_ref[0] = _dot3(cnd * _sigmoid(cnd), w_ref[0]) + b_ref[0]


def _modulation(cond, w_mod, b_mod):
    depth, d, d3 = w_mod.shape
    n = cond.shape[0]
    return pl.pallas_call(
        _mod_kernel,
        grid=(depth, d3 // d),
        in_specs=[pl.BlockSpec((n, d), lambda l, j: (0, 0)),
                  pl.BlockSpec((1, d, d), lambda l, j: (l, 0, j)),
                  pl.BlockSpec((1, 1, d), lambda l, j: (l, 0, j))],
        out_specs=pl.BlockSpec((1, n, d), lambda l, j: (l, 0, j)),
        out_shape=jax.ShapeDtypeStruct((depth, n, d3), F32),
        name="modulation",
    )(cond, w_mod, b_mod.reshape(depth, 1, d3))


def _in_kernel(xc_ref, xl_ref, mod_ref, g_ref, w_ref, cos_ref, sin_ref, gq_ref, gk_ref, kkw_ref, rk_ref, ones_ref,
               p_ref, qo_ref, ko_ref, kko_ref, *, tiles_per_batch, ctx_tiles):
    d = xc_ref.shape[2]
    is_ctx = pl.program_id(0) % tiles_per_batch < ctx_tiles
    x = jnp.where(is_ctx, xc_ref[0], xl_ref[0])
    ms = jnp.mean(x * x, axis=-1, keepdims=True)
    y = x * lax.rsqrt(ms + RMS_EPS) * g_ref[...]
    m = mod_ref[0]
    h = y * (1.0 + m[:, d:2 * d]) + m[:, 0:d]
    hb = h.astype(BF16)
    w = qo_ref.shape[1]
    n_head = 5 * w
    head = _dot(hb, w_ref[:, 0:n_head])
    p_ref[:, 0:3 * w] = head[:, 2 * w:].astype(BF16)

    ones = ones_ref[...]
    cos = cos_ref[...]
    sin = sin_ref[...]
    lane = lax.broadcasted_iota(jnp.int32, qo_ref.shape, 1)
    first = (lane & 31) < 16

    def norm_rope(t, g):
        msq = _group_sum(t * t, ones) * (1.0 / DH_C)
        tn = t * lax.rsqrt(msq + RMS_EPS) * g
        sw = jnp.where(first, pltpu.roll(tn, w - 16, 1), pltpu.roll(tn, 16, 1))
        return tn * cos + sw * sin

    def q_out():
        qo_ref[...] = (norm_rope(head[:, 0:w], gq_ref[...]) * (LOG2_E * DH_C ** -0.5)).astype(BF16)

    def k_out():
        ko_ref[...] = norm_rope(head[:, w:2 * w], gk_ref[...]).astype(BF16)

    def kk_out():
        kr = head[:, (2 + COL_K) * w:(3 + COL_K) * w] * kkw_ref[...]
        kko_ref[...] = (kr * lax.rsqrt(_group_sum(kr * kr, ones) + KK_EPS)).astype(BF16)

    def conv_out():
        res = _dot(hb, w_ref[:, (2 + COL_GB) * w:(2 + N_GROUPS) * w])
        r, k, v = (head[:, (2 + c) * w:(3 + c) * w] for c in (COL_R, COL_K, COL_V))
        p_ref[:, COL_GB * w:(COL_GB + 1) * w] = res[:, 0:w].astype(BF16)
        p_ref[:, COL_PC * w:(COL_PC + 1) * w] = (res[:, w:2 * w] * res[:, 2 * w:3 * w]).astype(BF16)
        p_ref[:, COL_BONUS * w:(COL_BONUS + 1) * w] = (_group_sum(r * k * rk_ref[...], ones) * v).astype(BF16)
        p_ref[:, COL_VD * w:(COL_VD + 1) * w] = res[:, 3 * w:4 * w].astype(BF16)

    def act_out(lo, hi, act):
        p_ref[:, lo:hi] = act(_dot(hb, w_ref[:, 2 * w + lo:2 * w + hi])).astype(BF16)

    silu = lambda z: z * _sigmoid(z)
    n_cols = p_ref.shape[1]
    blocks = ([functools.partial(act_out, COL_ZA * w, COL_GB * w, silu), conv_out]
              + [functools.partial(act_out, lo, lo + 2 * w, _sigmoid)
                 for lo in range(N_GROUPS * w, N_GROUPS * w + 3 * d, 2 * w)]
              + [functools.partial(act_out, N_GROUPS * w + 3 * d, n_cols, lambda z: z)])
    epilogue = [q_out, k_out, kk_out]
    for idx, block in enumerate(blocks):
        block()
        if idx < len(epilogue):
            epilogue[idx]()


def _stream_specs(d, tiles_per_batch, ctx_tiles):
    tile = lambda i: i % tiles_per_batch
    ctx = pl.BlockSpec((1, ROW_TILE, d), lambda i: (i // tiles_per_batch, jnp.minimum(tile(i), ctx_tiles - 1), 0))
    lat = pl.BlockSpec((1, ROW_TILE, d), lambda i: (i // tiles_per_batch, jnp.maximum(tile(i) - ctx_tiles, 0), 0))
    return ctx, lat


def _in_proj(xc, xl, mod_l, norm_g, w_in, cos_t, sin_t, gq, gk, k_k, r_k, ones_bd, mod_row, tiles_per_batch,
             ctx_tiles):
    bsz, _, d = xc.shape
    m = bsz * tiles_per_batch * ROW_TILE
    n = w_in.shape[1]
    w = BRANCH_W
    n_keep = n - 2 * w
    par = lambda shape: pl.BlockSpec(shape, lambda i: tuple(0 for _ in shape))
    tab = pl.BlockSpec((ROW_TILE, w), lambda i: (i % tiles_per_batch, 0))
    out = pl.BlockSpec((ROW_TILE, w), lambda i: (i, 0))
    kern = functools.partial(_in_kernel, tiles_per_batch=tiles_per_batch, ctx_tiles=ctx_tiles)
    return pl.pallas_call(
        kern,
        grid=(m // ROW_TILE,),
        in_specs=[*_stream_specs(d, tiles_per_batch, ctx_tiles),
                  pl.BlockSpec((1, 1, 3 * d), lambda i: (mod_row(i), 0, 0)),
                  par((1, d)),
                  pl.BlockSpec((d, n), lambda i: (0, 0), pipeline_mode=pl.Buffered(1)),
                  tab, tab, par((1, w)), par((1, w)), par((1, w)), par((1, w)), par(ones_bd.shape)],
        out_specs=[pl.BlockSpec((ROW_TILE, n_keep), lambda i: (i, 0)), out, out, out],
        out_shape=[jax.ShapeDtypeStruct((m, n_keep), BF16)] + [jax.ShapeDtypeStruct((m, w), BF16)] * 3,
        compiler_params=pltpu.CompilerParams(vmem_limit_bytes=VMEM_LIMIT),
        name="in_proj",
    )(xc, xl, mod_l, norm_g, w_in, cos_t, sin_t, gq, gk, k_k, r_k, ones_bd)


def _expand(z, lane_masks):
    zb = z.astype(BF16)
    zeros = jnp.zeros((z.shape[0], 2 * HEAD_A), BF16)
    blocks = []
    for hd in range(HEADS_PER_HALF):
        pair = hd // 2
        piece = zb[:, pair * 2 * HEAD_A:(pair + 1) * 2 * HEAD_A] * lane_masks[hd % 2]
        blocks.append(jnp.concatenate([piece if i == pair else zeros for i in range(HEADS_PER_HALF // 2)], axis=1))
    return jnp.concatenate(blocks, axis=0)


def _head_transpose(z):
    zt = z.T
    return jnp.concatenate([zt[hd * HEAD_A:(hd + 1) * HEAD_A] for hd in range(HEADS_PER_HALF)], axis=1)


def _rwkv_chunks(chains, lane_masks, eye, dir_masks):
    bf = lambda x: x.astype(BF16)
    ex = lambda z: _expand(z, lane_masks)
    stack = lambda *xs: jnp.concatenate([bf(x) for x in xs], axis=0)

    def prep(c):
        strict, incl, ltri = dir_masks[c["reverse"]]
        cum = _dot_exact_lhs(ltri, c["logw"])
        cum_end = cum[0:1] if c["reverse"] else cum[CHUNK - 1:CHUNK]
        kdir = c["k"] * (1.0 + (c["a"] - 1.0) * c["k_a"])
        akk = c["a"] * c["kk"]
        en = jnp.exp(-cum)
        ec = jnp.exp(cum_end - cum)
        c.update(strict=strict, incl=incl, kt=c["kk"] * jnp.exp(cum - c["logw"]), rt=c["r"] * jnp.exp(cum),
                 p_end=jnp.exp(cum_end), an_t=_head_transpose(akk * en), kn_t=_head_transpose(kdir * en),
                 ab_t=_head_transpose(akk * ec), kb_t=_head_transpose(kdir * ec))

    def gram(c):
        lhs = stack(c["kt"], c["rt"])
        g_a = _dot(lhs, ex(c["an_t"]))
        g_k = _dot(lhs, ex(c["kn_t"]))
        n = jnp.where(c["strict"], g_a[0:CHUNK], 0.0)
        c.update(n=n, t=jnp.where(eye, 1.0, 0.0) - n, b_ra=jnp.where(c["incl"], g_a[CHUNK:], 0.0),
                 a_kk=jnp.where(c["strict"], g_k[0:CHUNK], 0.0), b_rk=jnp.where(c["incl"], g_k[CHUNK:], 0.0))

    def square(c):
        c["npow"] = _dot(bf(c["n"]), ex(c["n"]))

    def inverse_step(c, last):
        w = ex(c["npow"])
        if last:
            c["t"] = c["t"] + _dot(bf(c["t"]), w)
        else:
            x = _dot(stack(c["t"], c["npow"]), w)
            c["t"] = c["t"] + x[0:CHUNK]
            c["npow"] = x[CHUNK:]

    def against_v(c):
        x = _dot(stack(c["a_kk"], c["b_rk"], c["kb_t"]), ex(c["v"]))
        c.update(w1=x[0:CHUNK], brk_v=x[CHUNK:2 * CHUNK], kbt_v=x[2 * CHUNK:])

    def apply_inverse(c):
        tb = bf(c["t"])
        c.update(khat=_dot(tb, ex(c["kt"])), uhat=_dot(tb, ex(c["w1"])))

    def chunk_maps(c):
        lhs = stack(c["b_ra"], c["ab_t"])
        xk = _dot(lhs, ex(c["khat"]))
        xu = _dot(lhs, ex(c["uhat"]))
        c.update(rhat=c["rt"] - xk[0:CHUNK], m=jnp.where(eye, c["p_end"], 0.0) - xk[CHUNK:],
                 yhat=c["brk_v"] - xu[0:CHUNK], g=c["kbt_v"] - xu[CHUNK:])

    def advance(c):
        x = _dot(stack(c["m"], c["rhat"]), ex(c["h0"]))
        c.update(y=x[CHUNK:] + c["yhat"], h1=x[0:CHUNK] + c["g"])

    stages = ([prep, gram, square] + [functools.partial(inverse_step, last=i == 4) for i in range(5)]
              + [against_v, apply_inverse, chunk_maps, advance])
    n_groups = min(RWKV_STAGGER, len(chains))
    groups = [chains[i::n_groups] for i in range(n_groups)]
    for tick in range(len(stages) + n_groups - 1):
        for gi, grp in enumerate(groups):
            if 0 <= tick - gi < len(stages):
                for c in grp:
                    stages[tick - gi](c)


def _rwkv_kernel(rkvf_ref, kkf_ref, llf_ref, rkvb_ref, kkb_ref, llb_ref,
                 w0_ref, dup_ref, a0_ref, aup_ref, ka_ref, yf_ref, yb_ref, h_ref):
    @pl.when(pl.program_id(1) == 0)
    def _():
        h_ref[...] = jnp.zeros(h_ref.shape, F32)

    first = lax.broadcasted_iota(jnp.int32, (CHUNK, 2 * HEAD_A), 1) < HEAD_A
    lane_masks = (jnp.where(first, 1.0, 0.0).astype(BF16), jnp.where(first, 0.0, 1.0).astype(BF16))
    tt = lax.broadcasted_iota(jnp.int32, (CHUNK, HALF_W), 0)
    ss = lax.broadcasted_iota(jnp.int32, (CHUNK, HALF_W), 1) & (CHUNK - 1)
    eye = tt == ss
    ti = lax.broadcasted_iota(jnp.int32, (CHUNK, CHUNK), 0)
    si = lax.broadcasted_iota(jnp.int32, (CHUNK, CHUNK), 1)
    dir_masks = {False: (ss < tt, ss <= tt, jnp.where(si <= ti, 1.0, 0.0).astype(BF16)),
                 True: (ss > tt, ss >= tt, jnp.where(si >= ti, 1.0, 0.0).astype(BF16))}
    group = rkvf_ref.shape[0]
    d_model_a = rkvf_ref.shape[2] // 3
    k_a = ka_ref[...]

    chains = []
    for d, (rkv_ref, kk_ref, ll_ref, y_ref) in enumerate(
            ((rkvf_ref, kkf_ref, llf_ref, yf_ref), (rkvb_ref, kkb_ref, llb_ref, yb_ref))):
        ll = ll_ref[...].reshape(group * CHUNK, ll_ref.shape[2])
        w_raw = w0_ref[d] + _dot(jnp.tanh(ll.astype(F32)).astype(BF16), dup_ref[d])
        logw_all = -EXP_M05 * _sigmoid(w_raw)
        a_all = _sigmoid(a0_ref[d] + _dot(ll, aup_ref[d]))
        for g in range(group):
            rows = slice(g * CHUNK, (g + 1) * CHUNK)
            rkv = rkv_ref[g].astype(F32)
            kk = kk_ref[g].astype(F32)
            for hh in range(d_model_a // HALF_W):
                sl = slice(hh * HALF_W, (hh + 1) * HALF_W)
                chains.append(dict(
                    r=rkv[:, hh * HALF_W:(hh + 1) * HALF_W],
                    k=rkv[:, d_model_a + hh * HALF_W:d_model_a + (hh + 1) * HALF_W],
                    v=rkv[:, 2 * d_model_a + hh * HALF_W:2 * d_model_a + (hh + 1) * HALF_W],
                    kk=kk[:, sl], logw=logw_all[rows, sl], a=a_all[rows, sl], k_a=k_a[:, sl],
                    h0=h_ref[g, d, :, sl], reverse=d == 1, out=(y_ref, g, d, sl)))
    _rwkv_chunks(chains, lane_masks, eye, dir_masks)
    for c in chains:
        y_ref, g, d, sl = c["out"]
        h_ref[g, d, :, sl] = c["h1"]
        y_ref[g, :, sl] = c["y"].astype(y_ref.dtype)


def _rwkv(p, kk, w0, dup, a0, aup, k_a, bsz, ctx_chunks, lat_chunks):
    n_all = p.shape[1]
    da = BRANCH_W
    nch = ctx_chunks + lat_chunks
    n_lowrank_col = (p.shape[2] - LOWRANK_W) // LOWRANK_W
    group = math.gcd(bsz, RWKV_GROUP)

    def bwd(i):
        return jnp.where(i < ctx_chunks, ctx_chunks - 1 - i, nch - 1 - (i - ctx_chunks))

    spec3 = lambda shape: pl.BlockSpec(shape, lambda b, i: (0, 0, 0))
    return pl.pallas_call(
        _rwkv_kernel,
        grid=(bsz // group, nch),
        in_specs=[pl.BlockSpec((group, CHUNK, 3 * da), lambda b, i: (b, i, 0)),
                  pl.BlockSpec((group, CHUNK, da), lambda b, i: (b, i, 0)),
                  pl.BlockSpec((group, CHUNK, LOWRANK_W), lambda b, i: (b, i, n_lowrank_col)),
                  pl.BlockSpec((group, CHUNK, 3 * da), lambda b, i: (b, bwd(i), 0)),
                  pl.BlockSpec((group, CHUNK, da), lambda b, i: (b, bwd(i), 0)),
                  pl.BlockSpec((group, CHUNK, LOWRANK_W), lambda b, i: (b, bwd(i), n_lowrank_col)),
                  spec3((2, 1, da)), spec3((2, LOWRANK_W, da)), spec3((2, 1, da)), spec3((2, LOWRANK_W, da)),
                  pl.BlockSpec((1, da), lambda b, i: (0, 0))],
        out_specs=[pl.BlockSpec((group, CHUNK, da), lambda b, i: (b, i, 0)),
                   pl.BlockSpec((group, CHUNK, da), lambda b, i: (b, bwd(i), 0))],
        out_shape=[jax.ShapeDtypeStruct((bsz, n_all, da), BF16)] * 2,
        scratch_shapes=[pltpu.VMEM((group, 2, CHUNK, da), F32)],
        compiler_params=pltpu.CompilerParams(vmem_limit_bytes=VMEM_LIMIT,
                                             dimension_semantics=("arbitrary", "arbitrary")),
        name="rwkv",
    )(p, kk, p, p, kk, p, w0, dup, a0, aup, k_a)


def _attn_kernel(q_ref, k_ref, v_ref, g_ref, lq_ref, sg_ref, li_ref, o_ref, p_scr, *, ctx_tiles, n_ctx):
    lq = lq_ref[...]
    li = li_ref[...]
    lam = (jnp.exp(jnp.sum(lq[0:1] * lq[1:2], axis=-1, keepdims=True))
           - jnp.exp(jnp.sum(lq[2:3] * lq[3:4], axis=-1, keepdims=True)) + li)
    n_heads = q_ref.shape[1] // DV_C
    n_all = k_ref.shape[0]
    lane = lax.broadcasted_iota(jnp.int32, (q_ref.shape[0], DV_C), 1)

    g = jnp.abs(g_ref[...])
    bound = (BOUND_MARGIN * LOG2_E * DH_C ** 0.5) * (jnp.max(g[0:1], axis=-1, keepdims=True)
                                                     * jnp.max(g[1:2], axis=-1, keepdims=True))

    def head_queries(h):
        q = q_ref[:, h * DV_C:(h + 1) * DV_C]
        zero = jnp.zeros_like(q)
        return jnp.where(lane < DH_C, q, zero), jnp.where(lane >= DH_C, q, zero)

    def finish(h, o):
        ms = jnp.mean(o * o, axis=-1, keepdims=True)
        o_ref[:, h * DV_C:(h + 1) * DV_C] = (o * lax.rsqrt(ms + RMS_EPS) * sg_ref[...] * (1.0 - li)).astype(BF16)

    def bounded_heads(n_keys):
        half = KEY_BLOCK // 2
        n_blk = n_keys // KEY_BLOCK
        blk = lambda j: slice(j * KEY_BLOCK, (j + 1) * KEY_BLOCK)
        cols = lambda h: slice(h * DV_C, (h + 1) * DV_C)
        state = [dict(run=[None, None]) for _ in range(n_heads)]

        def scores(h, j):
            st = state[h]
            if j == 0:
                st["qs"] = head_queries(h)
            for c in range(2):
                s = lax.dot_general(st["qs"][c], k_ref[blk(j), cols(h)], NT_DIMS, preferred_element_type=F32)
                e = jnp.exp2(s)
                p_scr[h, c, :, blk(j)] = e.astype(BF16)
                part = e[:, 0:half] + e[:, half:]
                st["run"][c] = part if st["run"][c] is None else st["run"][c] + part

        def row_sums(h):
            st = state[h]
            l0, l1 = (jnp.sum(st["run"][c], axis=-1, keepdims=True) for c in range(2))
            st["inv_l0"] = 1.0 / l0
            st["ratio"] = (lam * l0 / l1).astype(BF16)
            st["acc"] = None

        def values(h, lo, hi):
            st = state[h]
            keys = slice(lo * KEY_BLOCK, hi * KEY_BLOCK)
            attn = p_scr[h, 0, :, keys] - p_scr[h, 1, :, keys] * st["ratio"]
            o = _dot(attn, v_ref[keys, cols(h)])
            st["acc"] = o if st["acc"] is None else st["acc"] + o

        n_chunks = min(VALUE_CHUNKS, n_blk)
        bounds = [round(i * n_blk / n_chunks) for i in range(n_chunks + 1)]
        for j in range(n_blk):
            scores(0, j)
        row_sums(0)
        for h in range(n_heads):
            for ci in range(n_chunks):
                if h + 1 < n_heads:
                    for j in range(bounds[ci], bounds[ci + 1]):
                        scores(h + 1, j)
                values(h, bounds[ci], bounds[ci + 1])
            if h + 1 < n_heads:
                row_sums(h + 1)
            finish(h, state[h]["acc"] * state[h]["inv_l0"])

    def attend_exact(h, n_keys):
        cols = slice(h * DV_C, (h + 1) * DV_C)
        qs = head_queries(h)
        k = k_ref[0:n_keys, cols]

        def soft(qc):
            s = lax.dot_general(qc, k, NT_DIMS, preferred_element_type=F32)
            e = jnp.exp2(s - jnp.max(s, axis=-1, keepdims=True))
            return e, 1.0 / jnp.sum(e, axis=-1, keepdims=True)

        e0, i0 = soft(qs[0])
        e1, i1 = soft(qs[1])
        attn = e0 * i0 - e1 * (lam * i1)
        finish(h, _dot(attn.astype(BF16), v_ref[0:n_keys, cols]))

    is_ctx = pl.program_id(2) < ctx_tiles
    bounded = jnp.max(bound) < MAX_SAFE_LOG2
    for n_keys, tile_pred in ((n_ctx, is_ctx), (n_all, jnp.logical_not(is_ctx))):
        @pl.when(jnp.logical_and(tile_pred, bounded))
        def _(n_keys=n_keys):
            bounded_heads(n_keys)

        @pl.when(jnp.logical_and(tile_pred, jnp.logical_not(bounded)))
        def _(n_keys=n_keys):
            for h in range(n_heads):
                attend_exact(h, n_keys)


def _attn(qh, kh, p, qk_g, lambda_qk, subln_g, lam_init, bsz, n_ctx, n_all):
    m = qh.shape[0]
    width = ATTN_HEADS * DV_C
    n_groups = qh.shape[1] // width
    tiles = n_all // ROW_TILE
    vcol = COL_VD * BRANCH_W // width
    kern = functools.partial(_attn_kernel, ctx_tiles=n_ctx // ROW_TILE, n_ctx=n_ctx)
    return pl.pallas_call(
        kern,
        grid=(bsz, n_groups, tiles),
        in_specs=[pl.BlockSpec((ROW_TILE, width), lambda b, h, i: (b * tiles + i, h)),
                  pl.BlockSpec((n_all, width), lambda b, h, i: (b, h)),
                  pl.BlockSpec((n_all, width), lambda b, h, i: (b, vcol + h)),
                  pl.BlockSpec(qk_g.shape, lambda b, h, i: (0, 0)),
                  pl.BlockSpec(lambda_qk.shape, lambda b, h, i: (0, 0)),
                  pl.BlockSpec((1, DV_C), lambda b, h, i: (0, 0)),
                  pl.BlockSpec((1, 1), lambda b, h, i: (0, 0))],
        out_specs=pl.BlockSpec((ROW_TILE, width), lambda b, h, i: (b * tiles + i, h)),
        out_shape=jax.ShapeDtypeStruct((m, qh.shape[1]), BF16),
        scratch_shapes=[pltpu.VMEM((ATTN_HEADS, 2, ROW_TILE, n_all), BF16)],
        compiler_params=pltpu.CompilerParams(vmem_limit_bytes=VMEM_LIMIT),
        name="diff_attn",
    )(qh, kh, p, qk_g, lambda_qk, subln_g, lam_init)


def _merge_kernel(xc_ref, xl_ref, mod_ref, yf_ref, yb_ref, cv_ref, pcp_ref, pcn_ref,
                  zs_ref, yc_ref, gm0_ref, gm1_ref, gm2_ref, gng_ref, gnb_ref, cw_ref, cb_ref, wbr_ref,
                  wout_ref, ones_ref, oc_ref, ol_ref, *, tiles_per_batch, ctx_tiles):
    d = xc_ref.shape[2]
    bw = yf_ref.shape[1]
    ones = ones_ref[...]
    t = pl.program_id(0) % tiles_per_batch
    seg_start = jnp.logical_or(t == 0, t == ctx_tiles)
    seg_end = jnp.logical_or(t == ctx_tiles - 1, t == tiles_per_batch - 1)

    y = yf_ref[...].astype(F32) + yb_ref[...].astype(F32)
    mu = _group_sum(y, ones) * (1.0 / HEAD_A)
    dev = y - mu
    var = _group_sum(dev * dev, ones) * (1.0 / HEAD_A)
    cv = cv_ref[...].astype(F32)
    gb, pc, bonus = cv[:, 0:bw], cv[:, bw:2 * bw], cv[:, 2 * bw:3 * bw]
    y_a = dev * lax.rsqrt(var + GN_EPS) * gng_ref[...] + gnb_ref[...] + bonus

    hrows = pcp_ref.shape[0]
    prev_row = pcp_ref[...].astype(F32)[hrows - 1:hrows]
    next_row = pcn_ref[...].astype(F32)[0:1]
    prev_row = jnp.where(seg_start, 0.0, prev_row)
    next_row = jnp.where(seg_end, 0.0, next_row)
    rows = pc.shape[0]
    row = lax.broadcasted_iota(jnp.int32, pc.shape, 0)
    p_prev = jnp.where(row == 0, prev_row, pltpu.roll(pc, 1, 0))
    p_next = jnp.where(row == rows - 1, next_row, pltpu.roll(pc, rows - 1, 0))
    cw = cw_ref[...]
    y_b = gb * (p_prev * cw[0:1] + pc * cw[1:2] + p_next * cw[2:3] + cb_ref[...])

    zs = zs_ref[...].astype(F32)
    acc = jnp.zeros((rows, d), F32)
    for idx, (yy, gm_ref) in enumerate(zip((y_a, y_b, yc_ref[...].astype(F32)), (gm0_ref, gm1_ref, gm2_ref))):
        br = _dot((yy * zs[:, idx * bw:(idx + 1) * bw]).astype(BF16), wbr_ref[idx])
        acc = acc + gm_ref[...].astype(F32) * br
    out = _dot(acc.astype(BF16), wout_ref[...])
    gate = mod_ref[0][:, 2 * d:3 * d]
    is_ctx = t < ctx_tiles
    res = jnp.where(is_ctx, xc_ref[0], xl_ref[0]) + gate * out

    @pl.when(is_ctx)
    def _():
        oc_ref[0] = res

    @pl.when(jnp.logical_not(is_ctx))
    def _():
        ol_ref[0] = res


def _merge(xc, xl, mod_l, yf, yb, p, yc, gn_g, gn_b, conv_w, conv_b, w_branch, w_out, ones_bd,
           mod_row, tiles_per_batch, ctx_tiles):
    bsz, _, d = xc.shape
    m = bsz * tiles_per_batch * ROW_TILE
    bw = BRANCH_W
    halo = HALO_ROWS
    per_tile = ROW_TILE // halo
    n_halo = m // halo
    col = lambda c, width=bw: pl.BlockSpec((ROW_TILE, width), lambda i, c=c: (i, c))
    par = lambda shape: pl.BlockSpec(shape, lambda i: tuple(0 for _ in shape))
    prev = lambda c: pl.BlockSpec((halo, bw), lambda i, c=c: (jnp.maximum(i * per_tile - 1, 0), c))
    nxt = lambda c: pl.BlockSpec((halo, bw), lambda i, c=c: (jnp.minimum((i + 1) * per_tile, n_halo - 1), c))
    gmcol = lambda c: pl.BlockSpec((ROW_TILE, d), lambda i, c=c: (i, N_GROUPS * bw // d + c))
    kern = functools.partial(_merge_kernel, tiles_per_batch=tiles_per_batch, ctx_tiles=ctx_tiles)
    return pl.pallas_call(
        kern,
        grid=(m // ROW_TILE,),
        in_specs=[*_stream_specs(d, tiles_per_batch, ctx_tiles),
                  pl.BlockSpec((1, 1, 3 * d), lambda i: (mod_row(i), 0, 0)),
                  col(0), col(0),
                  col(COL_GB // 3, 3 * bw), prev(COL_PC), nxt(COL_PC),
                  col(COL_ZA // 3, 3 * bw), col(0), gmcol(0), gmcol(1), gmcol(2),
                  par((1, bw)), par((1, bw)), par(conv_w.shape), par((1, bw)),
                  par(w_branch.shape), par(w_out.shape), par(ones_bd.shape)],
        out_specs=list(_stream_specs(d, tiles_per_batch, ctx_tiles)),
        out_shape=[jax.ShapeDtypeStruct(xc.shape, F32), jax.ShapeDtypeStruct(xl.shape, F32)],
        compiler_params=pltpu.CompilerParams(vmem_limit_bytes=VMEM_LIMIT),
        name="merge",
    )(xc, xl, mod_l, yf, yb, p, p, p, p, yc, p, p, p,
      gn_g, gn_b, conv_w, conv_b, w_branch, w_out, ones_bd)


def _rope_tables(n_ctx, n_lat, width):
    rows = n_lat // GRID_W
    row = jnp.repeat(jnp.arange(rows), GRID_W).astype(F32)
    col = jnp.tile(jnp.arange(GRID_W), rows).astype(F32)
    n_freq = DH_C // 4
    inv_freq = ROPE_BASE ** (-jnp.arange(n_freq, dtype=F32) / n_freq)
    ar, ac = row[:, None] * inv_freq, col[:, None] * inv_freq
    cos = jnp.concatenate([jnp.cos(ar), jnp.cos(ar), jnp.cos(ac), jnp.cos(ac)], axis=-1)
    sin = jnp.concatenate([-jnp.sin(ar), jnp.sin(ar), -jnp.sin(ac), jnp.sin(ac)], axis=-1)
    cos = jnp.concatenate([jnp.ones((n_ctx, DH_C), F32), cos], axis=0)
    sin = jnp.concatenate([jnp.zeros((n_ctx, DH_C), F32), sin], axis=0)
    reps = width // DH_C
    return jnp.tile(cos, (1, reps)), jnp.tile(sin, (1, reps))


def _permute_w_in(w_in):
    a = BRANCH_W
    r, k, v = (w_in[..., i * a:(i + 1) * a] for i in range(3))
    low = w_in[..., 3 * a:3 * a + LOWRANK_W]
    rest = w_in[..., 3 * a + LOWRANK_W:]
    za, gb_gc_u, zb = rest[..., :a], rest[..., a:4 * a], rest[..., 4 * a:5 * a]
    qd_kd, vd, zc, gm = rest[..., 5 * a:7 * a], rest[..., 7 * a:8 * a], rest[..., 8 * a:9 * a], rest[..., 9 * a:]
    return jnp.concatenate([qd_kd, r, k, v, za, zb, zc, gb_gc_u, vd, gm, low], axis=-1)


def kernel(x, c, ctx, c_ctx, w_mod, b_mod, norm_g, w_in, decay_w0, decay_up, iclr_a0, iclr_up, k_k, k_a, r_k,
           gn_g, gn_b, conv_w, conv_b, qk_norm_g, lambda_qk, subln_g, w_branch, w_out):
    bsz, n_lat, d = x.shape
    n_ctx = ctx.shape[1]
    depth = w_mod.shape[0]
    n_all = n_ctx + n_lat
    assert n_ctx % ROW_TILE == 0 and n_lat % ROW_TILE == 0 and n_lat % GRID_W == 0
    tiles_per_batch = n_all // ROW_TILE
    ctx_tiles = n_ctx // ROW_TILE
    da = decay_w0.shape[-1]
    rank = decay_up.shape[-2]

    n_cond = 16
    assert bsz < n_cond
    cond = jnp.zeros((n_cond, d), F32).at[:bsz].set(c).at[bsz].set(c_ctx)
    mod = _modulation(cond, w_mod, b_mod).reshape(depth, n_cond, 1, 3 * d)

    def mod_row(i):
        return jnp.where(i % tiles_per_batch < ctx_tiles, bsz, i // tiles_per_batch)

    xc, xl = ctx, x
    w_in_p = _permute_w_in(w_in.astype(BF16))
    cos_t, sin_t = _rope_tables(n_ctx, n_lat, BRANCH_W)
    gi = lax.broadcasted_iota(jnp.int32, (HALF_W, HALF_W), 0) // HEAD_A
    gj = lax.broadcasted_iota(jnp.int32, (HALF_W, HALF_W), 1) // HEAD_A
    ones_bd = (gi == gj).astype(BF16)
    dup = jnp.zeros((depth, 2, 4 * rank, da), F32)
    aup = jnp.zeros((depth, 2, 4 * rank, da), F32)
    for dd in range(2):
        dup = dup.at[:, dd, dd * rank:(dd + 1) * rank].set(decay_up[:, dd])
        aup = aup.at[:, dd, (2 + dd) * rank:(3 + dd) * rank].set(iclr_up[:, dd])
    dup, aup = dup.astype(BF16), aup.astype(BF16)
    w_branch_b, w_out_b = w_branch.astype(BF16), w_out.astype(BF16)
    n_rep = BRANCH_W // DH_C

    for l in range(depth):
        lam_init = jnp.full((1, 1), 0.8 - 0.6 * math.exp(-0.3 * l), F32)
        p, qh, kh, kk = _in_proj(xc, xl, mod[l], norm_g[l].reshape(1, d), w_in_p[l], cos_t, sin_t,
                                 jnp.tile(qk_norm_g[l, 0], n_rep).reshape(1, BRANCH_W),
                                 jnp.tile(qk_norm_g[l, 1], n_rep).reshape(1, BRANCH_W), k_k[l].reshape(1, da),
                                 r_k[l].reshape(1, da), ones_bd, mod_row, tiles_per_batch, ctx_tiles)
        yf, yb = _rwkv(p.reshape(bsz, n_all, -1), kk.reshape(bsz, n_all, da), decay_w0[l].reshape(2, 1, da), dup[l],
                       iclr_a0[l].reshape(2, 1, da), aup[l], k_a[l].reshape(1, da), bsz, n_ctx // CHUNK,
                       n_lat // CHUNK)
        yf, yb = yf.reshape(bsz * n_all, da), yb.reshape(bsz * n_all, da)
        yc = _attn(qh, kh, p, qk_norm_g[l], lambda_qk[l], subln_g[l].reshape(1, DV_C), lam_init, bsz, n_ctx, n_all)
        xc, xl = _merge(xc, xl, mod[l], yf, yb, p, yc, gn_g[l].reshape(1, da),
                        gn_b[l].reshape(1, da), conv_w[l], conv_b[l].reshape(1, da), w_branch_b[l], w_out_b[l],
                        ones_bd, mod_row, tiles_per_batch, ctx_tiles)
    return xl
```

```python
import functools
import math

import jax
import jax.numpy as jnp
from jax import lax
from jax.experimental import pallas as pl
from jax.experimental.pallas import tpu as pltpu

F32 = jnp.float32
BF16 = jnp.bfloat16

ROW_TILE = 256
CHUNK = 64
RWKV_GROUP = 8
RWKV_STAGGER = 4
BRANCH_W = 512
LOWRANK_W = 256
HALO_ROWS = 16
HEAD_A = 64
HALF_W = 256
HEADS_PER_HALF = HALF_W // HEAD_A
DH_C = 64
DV_C = 128
KEY_BLOCK = 256
ATTN_HEADS = 4
VALUE_CHUNKS = 2
BOUND_MARGIN = 1.02
MAX_SAFE_LOG2 = 40.0
GRID_W = 64
ROPE_BASE = 10000.0
RMS_EPS = 1e-6
GN_EPS = 64e-5
KK_EPS = 1e-12
EXP_M05 = math.exp(-0.5)
LOG2_E = math.log2(math.e)
VMEM_LIMIT = 56 * 1024 * 1024

NT_DIMS = (((1,), (1,)), ((), ()))

COL_R, COL_K, COL_V, COL_ZA, COL_ZB, COL_ZC, COL_GB, COL_PC, COL_BONUS, COL_VD = range(10)
N_GROUPS = 10


def _dot(a, b):
    return jnp.dot(a, b, preferred_element_type=F32)


def _split_bf16(x):
    hi = x.astype(BF16)
    lo = (x - hi.astype(F32)).astype(BF16)
    return hi, lo


def _group_sum(x, ones_half):
    w = ones_half.shape[0]
    parts = [_dot(x[:, i:i + w].astype(BF16), ones_half) for i in range(0, x.shape[1], w)]
    return jnp.concatenate(parts, axis=1)


def _dot_exact_lhs(w_bf16, x):
    hi, lo = _split_bf16(x)
    return _dot(w_bf16, hi) + _dot(w_bf16, lo)


def _dot3(a, b):
    ah, al = _split_bf16(a)
    bh, bl = _split_bf16(b)
    return _dot(ah, bh) + _dot(ah, bl) + _dot(al, bh)


def _sigmoid(x):
    return 1.0 / (1.0 + jnp.exp(-x))


def _mod_kernel(cond_ref, w_ref, b_ref, o_ref):
    cnd = cond_ref[...]
    o_ref[0] = _dot3(cnd * _sigmoid(cnd), w_ref[0]) + b_ref[0]


def _modulation(cond, w_mod, b_mod):
    depth, d, d3 = w_mod.shape
    n = cond.shape[0]
    return pl.pallas_call(
        _mod_kernel,
        grid=(depth, d3 // d),
        in_specs=[pl.BlockSpec((n, d), lambda l, j: (0, 0)),
                  pl.BlockSpec((1, d, d), lambda l, j: (l, 0, j)),
                  pl.BlockSpec((1, 1, d), lambda l, j: (l, 0, j))],
        out_specs=pl.BlockSpec((1, n, d), lambda l, j: (l, 0, j)),
        out_shape=jax.ShapeDtypeStruct((depth, n, d3), F32),
        name="modulation",
    )(cond, w_mod, b_mod.reshape(depth, 1, d3))


def _in_kernel(xc_ref, xl_ref, mod_ref, g_ref, w_ref, cos_ref, sin_ref, gq_ref, gk_ref, kkw_ref, rk_ref, ones_ref,
               p_ref, qo_ref, ko_ref, kko_ref, *, tiles_per_batch, ctx_tiles):
    d = xc_ref.shape[2]
    is_ctx = pl.program_id(0) % tiles_per_batch < ctx_tiles
    x = jnp.where(is_ctx, xc_ref[0], xl_ref[0])
    ms = jnp.mean(x * x, axis=-1, keepdims=True)
    y = x * lax.rsqrt(ms + RMS_EPS) * g_ref[...]
    m = mod_ref[0]
    h = y * (1.0 + m[:, d:2 * d]) + m[:, 0:d]
    hb = h.astype(BF16)
    w = qo_ref.shape[1]
    n_head = 5 * w
    head = _dot(hb, w_ref[:, 0:n_head])
    p_ref[:, 0:3 * w] = head[:, 2 * w:].astype(BF16)

    ones = ones_ref[...]
    cos = cos_ref[...]
    sin = sin_ref[...]
    lane = lax.broadcasted_iota(jnp.int32, qo_ref.shape, 1)
    first = (lane & 31) < 16

    def norm_rope(t, g):
        msq = _group_sum(t * t, ones) * (1.0 / DH_C)
        tn = t * lax.rsqrt(msq + RMS_EPS) * g
        sw = jnp.where(first, pltpu.roll(tn, w - 16, 1), pltpu.roll(tn, 16, 1))
        return tn * cos + sw * sin

    def q_out():
        qo_ref[...] = (norm_rope(head[:, 0:w], gq_ref[...]) * (LOG2_E * DH_C ** -0.5)).astype(BF16)

    def k_out():
        ko_ref[...] = norm_rope(head[:, w:2 * w], gk_ref[...]).astype(BF16)

    def kk_out():
        kr = head[:, (2 + COL_K) * w:(3 + COL_K) * w] * kkw_ref[...]
        kko_ref[...] = (kr * lax.rsqrt(_group_sum(kr * kr, ones) + KK_EPS)).astype(BF16)

    def conv_out():
        res = _dot(hb, w_ref[:, (2 + COL_GB) * w:(2 + N_GROUPS) * w])
        r, k, v = (head[:, (2 + c) * w:(3 + c) * w] for c in (COL_R, COL_K, COL_V))
        p_ref[:, COL_GB * w:(COL_GB + 1) * w] = res[:, 0:w].astype(BF16)
        p_ref[:, COL_PC * w:(COL_PC + 1) * w] = (res[:, w:2 * w] * res[:, 2 * w:3 * w]).astype(BF16)
        p_ref[:, COL_BONUS * w:(COL_BONUS + 1) * w] = (_group_sum(r * k * rk_ref[...], ones) * v).astype(BF16)
        p_ref[:, COL_VD * w:(COL_VD + 1) * w] = res[:, 3 * w:4 * w].astype(BF16)

    def act_out(lo, hi, act):
        p_ref[:, lo:hi] = act(_dot(hb, w_ref[:, 2 * w + lo:2 * w + hi])).astype(BF16)

    silu = lambda z: z * _sigmoid(z)
    n_cols = p_ref.shape[1]
    blocks = ([functools.partial(act_out, COL_ZA * w, COL_GB * w, silu), conv_out]
              + [functools.partial(act_out, lo, lo + 2 * w, _sigmoid)
                 for lo in range(N_GROUPS * w, N_GROUPS * w + 3 * d, 2 * w)]
              + [functools.partial(act_out, N_GROUPS * w + 3 * d, n_cols, lambda z: z)])
    epilogue = [q_out, k_out, kk_out]
    for idx, block in enumerate(blocks):
        block()
        if idx < len(epilogue):
            epilogue[idx]()


def _stream_specs(d, tiles_per_batch, ctx_tiles):
    tile = lambda i: i % tiles_per_batch
    ctx = pl.BlockSpec((1, ROW_TILE, d), lambda i: (i // tiles_per_batch, jnp.minimum(tile(i), ctx_tiles - 1), 0))
    lat = pl.BlockSpec((1, ROW_TILE, d), lambda i: (i // tiles_per_batch, jnp.maximum(tile(i) - ctx_tiles, 0), 0))
    return ctx, lat


def _in_proj(xc, xl, mod_l, norm_g, w_in, cos_t, sin_t, gq, gk, k_k, r_k, ones_bd, mod_row, tiles_per_batch,
             ctx_tiles):
    bsz, _, d = xc.shape
    m = bsz * tiles_per_batch * ROW_TILE
    n = w_in.shape[1]
    w = BRANCH_W
    n_keep = n - 2 * w
    par = lambda shape: pl.BlockSpec(shape, lambda i: tuple(0 for _ in shape))
    tab = pl.BlockSpec((ROW_TILE, w), lambda i: (i % tiles_per_batch, 0))
    out = pl.BlockSpec((ROW_TILE, w), lambda i: (i, 0))
    kern = functools.partial(_in_kernel, tiles_per_batch=tiles_per_batch, ctx_tiles=ctx_tiles)
    return pl.pallas_call(
        kern,
        grid=(m // ROW_TILE,),
        in_specs=[*_stream_specs(d, tiles_per_batch, ctx_tiles),
                  pl.BlockSpec((1, 1, 3 * d), lambda i: (mod_row(i), 0, 0)),
                  par((1, d)),
                  pl.BlockSpec((d, n), lambda i: (0, 0), pipeline_mode=pl.Buffered(1)),
                  tab, tab, par((1, w)), par((1, w)), par((1, w)), par((1, w)), par(ones_bd.shape)],
        out_specs=[pl.BlockSpec((ROW_TILE, n_keep), lambda i: (i, 0)), out, out, out],
        out_shape=[jax.ShapeDtypeStruct((m, n_keep), BF16)] + [jax.ShapeDtypeStruct((m, w), BF16)] * 3,
        compiler_params=pltpu.CompilerParams(vmem_limit_bytes=VMEM_LIMIT),
        name="in_proj",
    )(xc, xl, mod_l, norm_g, w_in, cos_t, sin_t, gq, gk, k_k, r_k, ones_bd)


def _expand(z, lane_masks):
    zb = z.astype(BF16)
    zeros = jnp.zeros((z.shape[0], 2 * HEAD_A), BF16)
    blocks = []
    for hd in range(HEADS_PER_HALF):
        pair = hd // 2
        piece = zb[:, pair * 2 * HEAD_A:(pair + 1) * 2 * HEAD_A] * lane_masks[hd % 2]
        blocks.append(jnp.concatenate([piece if i == pair else zeros for i in range(HEADS_PER_HALF // 2)], axis=1))
    return jnp.concatenate(blocks, axis=0)


def _head_transpose(z):
    zt = z.T
    return jnp.concatenate([zt[hd * HEAD_A:(hd + 1) * HEAD_A] for hd in range(HEADS_PER_HALF)], axis=1)


def _rwkv_chunks(chains, lane_masks, eye, dir_masks):
    bf = lambda x: x.astype(BF16)
    ex = lambda z: _expand(z, lane_masks)
    stack = lambda *xs: jnp.concatenate([bf(x) for x in xs], axis=0)

    def prep(c):
        strict, incl, ltri = dir_masks[c["reverse"]]
        cum = _dot_exact_lhs(ltri, c["logw"])
        cum_end = cum[0:1] if c["reverse"] else cum[CHUNK - 1:CHUNK]
        kdir = c["k"] * (1.0 + (c["a"] - 1.0) * c["k_a"])
        akk = c["a"] * c["kk"]
        en = jnp.exp(-cum)
        ec = jnp.exp(cum_end - cum)
        c.update(strict=strict, incl=incl, kt=c["kk"] * jnp.exp(cum - c["logw"]), rt=c["r"] * jnp.exp(cum),
                 p_end=jnp.exp(cum_end), an_t=_head_transpose(akk * en), kn_t=_head_transpose(kdir * en),
                 ab_t=_head_transpose(akk * ec), kb_t=_head_transpose(kdir * ec))

    def gram(c):
        lhs = stack(c["kt"], c["rt"])
        g_a = _dot(lhs, ex(c["an_t"]))
        g_k = _dot(lhs, ex(c["kn_t"]))
        n = jnp.where(c["strict"], g_a[0:CHUNK], 0.0)
        c.update(n=n, t=jnp.where(eye, 1.0, 0.0) - n, b_ra=jnp.where(c["incl"], g_a[CHUNK:], 0.0),
                 a_kk=jnp.where(c["strict"], g_k[0:CHUNK], 0.0), b_rk=jnp.where(c["incl"], g_k[CHUNK:], 0.0))

    def square(c):
        c["npow"] = _dot(bf(c["n"]), ex(c["n"]))

    def inverse_step(c, last):
        w = ex(c["npow"])
        if last:
            c["t"] = c["t"] + _dot(bf(c["t"]), w)
        else:
            x = _dot(stack(c["t"], c["npow"]), w)
            c["t"] = c["t"] + x[0:CHUNK]
            c["npow"] = x[CHUNK:]

    def against_v(c):
        x = _dot(stack(c["a_kk"], c["b_rk"], c["kb_t"]), ex(c["v"]))
        c.update(w1=x[0:CHUNK], brk_v=x[CHUNK:2 * CHUNK], kbt_v=x[2 * CHUNK:])

    def apply_inverse(c):
        tb = bf(c["t"])
        c.update(khat=_dot(tb, ex(c["kt"])), uhat=_dot(tb, ex(c["w1"])))

    def chunk_maps(c):
        lhs = stack(c["b_ra"], c["ab_t"])
        xk = _dot(lhs, ex(c["khat"]))
        xu = _dot(lhs, ex(c["uhat"]))
        c.update(rhat=c["rt"] - xk[0:CHUNK], m=jnp.where(eye, c["p_end"], 0.0) - xk[CHUNK:],
                 yhat=c["brk_v"] - xu[0:CHUNK], g=c["kbt_v"] - xu[CHUNK:])

    def advance(c):
        x = _dot(stack(c["m"], c["rhat"]), ex(c["h0"]))
        c.update(y=x[CHUNK:] + c["yhat"], h1=x[0:CHUNK] + c["g"])

    stages = ([prep, gram, square] + [functools.partial(inverse_step, last=i == 4) for i in range(5)]
              + [against_v, apply_inverse, chunk_maps, advance])
    n_groups = min(RWKV_STAGGER, len(chains))
    groups = [chains[i::n_groups] for i in range(n_groups)]
    for tick in range(len(stages) + n_groups - 1):
        for gi, grp in enumerate(groups):
            if 0 <= tick - gi < len(stages):
                for c in grp:
                    stages[tick - gi](c)


def _rwkv_kernel(rkvf_ref, kkf_ref, llf_ref, rkvb_ref, kkb_ref, llb_ref,
                 w0_ref, dup_ref, a0_ref, aup_ref, ka_ref, yf_ref, yb_ref, h_ref):
    @pl.when(pl.program_id(1) == 0)
    def _():
        h_ref[...] = jnp.zeros(h_ref.shape, F32)

    first = lax.broadcasted_iota(jnp.int32, (CHUNK, 2 * HEAD_A), 1) < HEAD_A
    lane_masks = (jnp.where(first, 1.0, 0.0).astype(BF16), jnp.where(first, 0.0, 1.0).astype(BF16))
    tt = lax.broadcasted_iota(jnp.int32, (CHUNK, HALF_W), 0)
    ss = lax.broadcasted_iota(jnp.int32, (CHUNK, HALF_W), 1) & (CHUNK - 1)
    eye = tt == ss
    ti = lax.broadcasted_iota(jnp.int32, (CHUNK, CHUNK), 0)
    si = lax.broadcasted_iota(jnp.int32, (CHUNK, CHUNK), 1)
    dir_masks = {False: (ss < tt, ss <= tt, jnp.where(si <= ti, 1.0, 0.0).astype(BF16)),
                 True: (ss > tt, ss >= tt, jnp.where(si >= ti, 1.0, 0.0).astype(BF16))}
    group = rkvf_ref.shape[0]
    d_model_a = rkvf_ref.shape[2] // 3
    k_a = ka_ref[...]

    chains = []
    for d, (rkv_ref, kk_ref, ll_ref, y_ref) in enumerate(
            ((rkvf_ref, kkf_ref, llf_ref, yf_ref), (rkvb_ref, kkb_ref, llb_ref, yb_ref))):
        ll = ll_ref[...].reshape(group * CHUNK, ll_ref.shape[2])
        w_raw = w0_ref[d] + _dot(jnp.tanh(ll.astype(F32)).astype(BF16), dup_ref[d])
        logw_all = -EXP_M05 * _sigmoid(w_raw)
        a_all = _sigmoid(a0_ref[d] + _dot(ll, aup_ref[d]))
        for g in range(group):
            rows = slice(g * CHUNK, (g + 1) * CHUNK)
            rkv = rkv_ref[g].astype(F32)
            kk = kk_ref[g].astype(F32)
            for hh in range(d_model_a // HALF_W):
                sl = slice(hh * HALF_W, (hh + 1) * HALF_W)
                chains.append(dict(
                    r=rkv[:, hh * HALF_W:(hh + 1) * HALF_W],
                    k=rkv[:, d_model_a + hh * HALF_W:d_model_a + (hh + 1) * HALF_W],
                    v=rkv[:, 2 * d_model_a + hh * HALF_W:2 * d_model_a + (hh + 1) * HALF_W],
                    kk=kk[:, sl], logw=logw_all[rows, sl], a=a_all[rows, sl], k_a=k_a[:, sl],
                    h0=h_ref[g, d, :, sl], reverse=d == 1, out=(y_ref, g, d, sl)))
    _rwkv_chunks(chains, lane_masks, eye, dir_masks)
    for c in chains:
        y_ref, g, d, sl = c["out"]
        h_ref[g, d, :, sl] = c["h1"]
        y_ref[g, :, sl] = c["y"].astype(y_ref.dtype)


def _rwkv(p, kk, w0, dup, a0, aup, k_a, bsz, ctx_chunks, lat_chunks):
    n_all = p.shape[1]
    da = BRANCH_W
    nch = ctx_chunks + lat_chunks
    n_lowrank_col = (p.shape[2] - LOWRANK_W) // LOWRANK_W
    group = math.gcd(bsz, RWKV_GROUP)

    def bwd(i):
        return jnp.where(i < ctx_chunks, ctx_chunks - 1 - i, nch - 1 - (i - ctx_chunks))

    spec3 = lambda shape: pl.BlockSpec(shape, lambda b, i: (0, 0, 0))
    return pl.pallas_call(
        _rwkv_kernel,
        grid=(bsz // group, nch),
        in_specs=[pl.BlockSpec((group, CHUNK, 3 * da), lambda b, i: (b, i, 0)),
                  pl.BlockSpec((group, CHUNK, da), lambda b, i: (b, i, 0)),
                  pl.BlockSpec((group, CHUNK, LOWRANK_W), lambda b, i: (b, i, n_lowrank_col)),
                  pl.BlockSpec((group, CHUNK, 3 * da), lambda b, i: (b, bwd(i), 0)),
                  pl.BlockSpec((group, CHUNK, da), lambda b, i: (b, bwd(i), 0)),
                  pl.BlockSpec((group, CHUNK, LOWRANK_W), lambda b, i: (b, bwd(i), n_lowrank_col)),
                  spec3((2, 1, da)), spec3((2, LOWRANK_W, da)), spec3((2, 1, da)), spec3((2, LOWRANK_W, da)),
                  pl.BlockSpec((1, da), lambda b, i: (0, 0))],
        out_specs=[pl.BlockSpec((group, CHUNK, da), lambda b, i: (b, i, 0)),
                   pl.BlockSpec((group, CHUNK, da), lambda b, i: (b, bwd(i), 0))],
        out_shape=[jax.ShapeDtypeStruct((bsz, n_all, da), BF16)] * 2,
        scratch_shapes=[pltpu.VMEM((group, 2, CHUNK, da), F32)],
        compiler_params=pltpu.CompilerParams(vmem_limit_bytes=VMEM_LIMIT,
                                             dimension_semantics=("arbitrary", "arbitrary")),
        name="rwkv",
    )(p, kk, p, p, kk, p, w0, dup, a0, aup, k_a)


def _attn_kernel(q_ref, k_ref, v_ref, g_ref, lq_ref, sg_ref, li_ref, o_ref, p_scr, *, ctx_tiles, n_ctx):
    lq = lq_ref[...]
    li = li_ref[...]
    lam = (jnp.exp(jnp.sum(lq[0:1] * lq[1:2], axis=-1, keepdims=True))
           - jnp.exp(jnp.sum(lq[2:3] * lq[3:4], axis=-1, keepdims=True)) + li)
    n_heads = q_ref.shape[1] // DV_C
    n_all = k_ref.shape[0]
    lane = lax.broadcasted_iota(jnp.int32, (q_ref.shape[0], DV_C), 1)

    g = jnp.abs(g_ref[...])
    bound = (BOUND_MARGIN * LOG2_E * DH_C ** 0.5) * (jnp.max(g[0:1], axis=-1, keepdims=True)
                                                     * jnp.max(g[1:2], axis=-1, keepdims=True))

    def head_queries(h):
        q = q_ref[:, h * DV_C:(h + 1) * DV_C]
        zero = jnp.zeros_like(q)
        return jnp.where(lane < DH_C, q, zero), jnp.where(lane >= DH_C, q, zero)

    def finish(h, o):
        ms = jnp.mean(o * o, axis=-1, keepdims=True)
        o_ref[:, h * DV_C:(h + 1) * DV_C] = (o * lax.rsqrt(ms + RMS_EPS) * sg_ref[...] * (1.0 - li)).astype(BF16)

    def bounded_heads(n_keys):
        half = KEY_BLOCK // 2
        n_blk = n_keys // KEY_BLOCK
        blk = lambda j: slice(j * KEY_BLOCK, (j + 1) * KEY_BLOCK)
        cols = lambda h: slice(h * DV_C, (h + 1) * DV_C)
        state = [dict(run=[None, None]) for _ in range(n_heads)]

        def scores(h, j):
            st = state[h]
            if j == 0:
                st["qs"] = head_queries(h)
            for c in range(2):
                s = lax.dot_general(st["qs"][c], k_ref[blk(j), cols(h)], NT_DIMS, preferred_element_type=F32)
                e = jnp.exp2(s)
                p_scr[h, c, :, blk(j)] = e.astype(BF16)
                part = e[:, 0:half] + e[:, half:]
                st["run"][c] = part if st["run"][c] is None else st["run"][c] + part

        def row_sums(h):
            st = state[h]
            l0, l1 = (jnp.sum(st["run"][c], axis=-1, keepdims=True) for c in range(2))
            st["inv_l0"] = 1.0 / l0
            st["ratio"] = (lam * l0 / l1).astype(BF16)
            st["acc"] = None

        def values(h, lo, hi):
            st = state[h]
            keys = slice(lo * KEY_BLOCK, hi * KEY_BLOCK)
            attn = p_scr[h, 0, :, keys] - p_scr[h, 1, :, keys] * st["ratio"]
            o = _dot(attn, v_ref[keys, cols(h)])
            st["acc"] = o if st["acc"] is None else st["acc"] + o

        n_chunks = min(VALUE_CHUNKS, n_blk)
        bounds = [round(i * n_blk / n_chunks) for i in range(n_chunks + 1)]
        for j in range(n_blk):
            scores(0, j)
        row_sums(0)
        for h in range(n_heads):
            for ci in range(n_chunks):
                if h + 1 < n_heads:
                    for j in range(bounds[ci], bounds[ci + 1]):
                        scores(h + 1, j)
                values(h, bounds[ci], bounds[ci + 1])
            if h + 1 < n_heads:
                row_sums(h + 1)
            finish(h, state[h]["acc"] * state[h]["inv_l0"])

    def attend_exact(h, n_keys):
        cols = slice(h * DV_C, (h + 1) * DV_C)
        qs = head_queries(h)
        k = k_ref[0:n_keys, cols]

        def soft(qc):
            s = lax.dot_general(qc, k, NT_DIMS, preferred_element_type=F32)
            e = jnp.exp2(s - jnp.max(s, axis=-1, keepdims=True))
            return e, 1.0 / jnp.sum(e, axis=-1, keepdims=True)

        e0, i0 = soft(qs[0])
        e1, i1 = soft(qs[1])
        attn = e0 * i0 - e1 * (lam * i1)
        finish(h, _dot(attn.astype(BF16), v_ref[0:n_keys, cols]))

    is_ctx = pl.program_id(2) < ctx_tiles
    bounded = jnp.max(bound) < MAX_SAFE_LOG2
    for n_keys, tile_pred in ((n_ctx, is_ctx), (n_all, jnp.logical_not(is_ctx))):
        @pl.when(jnp.logical_and(tile_pred, bounded))
        def _(n_keys=n_keys):
            bounded_heads(n_keys)

        @pl.when(jnp.logical_and(tile_pred, jnp.logical_not(bounded)))
        def _(n_keys=n_keys):
            for h in range(n_heads):
                attend_exact(h, n_keys)


def _attn(qh, kh, p, qk_g, lambda_qk, subln_g, lam_init, bsz, n_ctx, n_all):
    m = qh.shape[0]
    width = ATTN_HEADS * DV_C
    n_groups = qh.shape[1] // width
    tiles = n_all // ROW_TILE
    vcol = COL_VD * BRANCH_W // width
    kern = functools.partial(_attn_kernel, ctx_tiles=n_ctx // ROW_TILE, n_ctx=n_ctx)
    return pl.pallas_call(
        kern,
        grid=(bsz, n_groups, tiles),
        in_specs=[pl.BlockSpec((ROW_TILE, width), lambda b, h, i: (b * tiles + i, h)),
                  pl.BlockSpec((n_all, width), lambda b, h, i: (b, h)),
                  pl.BlockSpec((n_all, width), lambda b, h, i: (b, vcol + h)),
                  pl.BlockSpec(qk_g.shape, lambda b, h, i: (0, 0)),
                  pl.BlockSpec(lambda_qk.shape, lambda b, h, i: (0, 0)),
                  pl.BlockSpec((1, DV_C), lambda b, h, i: (0, 0)),
                  pl.BlockSpec((1, 1), lambda b, h, i: (0, 0))],
        out_specs=pl.BlockSpec((ROW_TILE, width), lambda b, h, i: (b * tiles + i, h)),
        out_shape=jax.ShapeDtypeStruct((m, qh.shape[1]), BF16),
        scratch_shapes=[pltpu.VMEM((ATTN_HEADS, 2, ROW_TILE, n_all), BF16)],
        compiler_params=pltpu.CompilerParams(vmem_limit_bytes=VMEM_LIMIT),
        name="diff_attn",
    )(qh, kh, p, qk_g, lambda_qk, subln_g, lam_init)


def _merge_kernel(xc_ref, xl_ref, mod_ref, yf_ref, yb_ref, cv_ref, pcp_ref, pcn_ref,
                  zs_ref, yc_ref, gm0_ref, gm1_ref, gm2_ref, gng_ref, gnb_ref, cw_ref, cb_ref, wbr_ref,
                  wout_ref, ones_ref, oc_ref, ol_ref, *, tiles_per_batch, ctx_tiles):
    d = xc_ref.shape[2]
    bw = yf_ref.shape[1]
    ones = ones_ref[...]
    t = pl.program_id(0) % tiles_per_batch
    seg_start = jnp.logical_or(t == 0, t == ctx_tiles)
    seg_end = jnp.logical_or(t == ctx_tiles - 1, t == tiles_per_batch - 1)

    y = yf_ref[...].astype(F32) + yb_ref[...].astype(F32)
    mu = _group_sum(y, ones) * (1.0 / HEAD_A)
    dev = y - mu
    var = _group_sum(dev * dev, ones) * (1.0 / HEAD_A)
    cv = cv_ref[...].astype(F32)
    gb, pc, bonus = cv[:, 0:bw], cv[:, bw:2 * bw], cv[:, 2 * bw:3 * bw]
    y_a = dev * lax.rsqrt(var + GN_EPS) * gng_ref[...] + gnb_ref[...] + bonus

    hrows = pcp_ref.shape[0]
    prev_row = pcp_ref[...].astype(F32)[hrows - 1:hrows]
    next_row = pcn_ref[...].astype(F32)[0:1]
    prev_row = jnp.where(seg_start, 0.0, prev_row)
    next_row = jnp.where(seg_end, 0.0, next_row)
    rows = pc.shape[0]
    row = lax.broadcasted_iota(jnp.int32, pc.shape, 0)
    p_prev = jnp.where(row == 0, prev_row, pltpu.roll(pc, 1, 0))
    p_next = jnp.where(row == rows - 1, next_row, pltpu.roll(pc, rows - 1, 0))
    cw = cw_ref[...]
    y_b = gb * (p_prev * cw[0:1] + pc * cw[1:2] + p_next * cw[2:3] + cb_ref[...])

    zs = zs_ref[...].astype(F32)
    acc = jnp.zeros((rows, d), F32)
    for idx, (yy, gm_ref) in enumerate(zip((y_a, y_b, yc_ref[...].astype(F32)), (gm0_ref, gm1_ref, gm2_ref))):
        br = _dot((yy * zs[:, idx * bw:(idx + 1) * bw]).astype(BF16), wbr_ref[idx])
        acc = acc + gm_ref[...].astype(F32) * br
    out = _dot(acc.astype(BF16), wout_ref[...])
    gate = mod_ref[0][:, 2 * d:3 * d]
    is_ctx = t < ctx_tiles
    res = jnp.where(is_ctx, xc_ref[0], xl_ref[0]) + gate * out

    @pl.when(is_ctx)
    def _():
        oc_ref[0] = res

    @pl.when(jnp.logical_not(is_ctx))
    def _():
        ol_ref[0] = res


def _merge(xc, xl, mod_l, yf, yb, p, yc, gn_g, gn_b, conv_w, conv_b, w_branch, w_out, ones_bd,
           mod_row, tiles_per_batch, ctx_tiles):
    bsz, _, d = xc.shape
    m = bsz * tiles_per_batch * ROW_TILE
    bw = BRANCH_W
    halo = HALO_ROWS
    per_tile = ROW_TILE // halo
    n_halo = m // halo
    col = lambda c, width=bw: pl.BlockSpec((ROW_TILE, width), lambda i, c=c: (i, c))
    par = lambda shape: pl.BlockSpec(shape, lambda i: tuple(0 for _ in shape))
    prev = lambda c: pl.BlockSpec((halo, bw), lambda i, c=c: (jnp.maximum(i * per_tile - 1, 0), c))
    nxt = lambda c: pl.BlockSpec((halo, bw), lambda i, c=c: (jnp.minimum((i + 1) * per_tile, n_halo - 1), c))
    gmcol = lambda c: pl.BlockSpec((ROW_TILE, d), lambda i, c=c: (i, N_GROUPS * bw // d + c))
    kern = functools.partial(_merge_kernel, tiles_per_batch=tiles_per_batch, ctx_tiles=ctx_tiles)
    return pl.pallas_call(
        kern,
        grid=(m // ROW_TILE,),
        in_specs=[*_stream_specs(d, tiles_per_batch, ctx_tiles),
                  pl.BlockSpec((1, 1, 3 * d), lambda i: (mod_row(i), 0, 0)),
                  col(0), col(0),
                  col(COL_GB // 3, 3 * bw), prev(COL_PC), nxt(COL_PC),
                  col(COL_ZA // 3, 3 * bw), col(0), gmcol(0), gmcol(1), gmcol(2),
                  par((1, bw)), par((1, bw)), par(conv_w.shape), par((1, bw)),
                  par(w_branch.shape), par(w_out.shape), par(ones_bd.shape)],
        out_specs=list(_stream_specs(d, tiles_per_batch, ctx_tiles)),
        out_shape=[jax.ShapeDtypeStruct(xc.shape, F32), jax.ShapeDtypeStruct(xl.shape, F32)],
        compiler_params=pltpu.CompilerParams(vmem_limit_bytes=VMEM_LIMIT),
        name="merge",
    )(xc, xl, mod_l, yf, yb, p, p, p, p, yc, p, p, p,
      gn_g, gn_b, conv_w, conv_b, w_branch, w_out, ones_bd)


def _rope_tables(n_ctx, n_lat, width):
    rows = n_lat // GRID_W
    row = jnp.repeat(jnp.arange(rows), GRID_W).astype(F32)
    col = jnp.tile(jnp.arange(GRID_W), rows).astype(F32)
    n_freq = DH_C // 4
    inv_freq = ROPE_BASE ** (-jnp.arange(n_freq, dtype=F32) / n_freq)
    ar, ac = row[:, None] * inv_freq, col[:, None] * inv_freq
    cos = jnp.concatenate([jnp.cos(ar), jnp.cos(ar), jnp.cos(ac), jnp.cos(ac)], axis=-1)
    sin = jnp.concatenate([-jnp.sin(ar), jnp.sin(ar), -jnp.sin(ac), jnp.sin(ac)], axis=-1)
    cos = jnp.concatenate([jnp.ones((n_ctx, DH_C), F32), cos], axis=0)
    sin = jnp.concatenate([jnp.zeros((n_ctx, DH_C), F32), sin], axis=0)
    reps = width // DH_C
    return jnp.tile(cos, (1, reps)), jnp.tile(sin, (1, reps))


def _permute_w_in(w_in):
    a = BRANCH_W
    r, k, v = (w_in[..., i * a:(i + 1) * a] for i in range(3))
    low = w_in[..., 3 * a:3 * a + LOWRANK_W]
    rest = w_in[..., 3 * a + LOWRANK_W:]
    za, gb_gc_u, zb = rest[..., :a], rest[..., a:4 * a], rest[..., 4 * a:5 * a]
    qd_kd, vd, zc, gm = rest[..., 5 * a:7 * a], rest[..., 7 * a:8 * a], rest[..., 8 * a:9 * a], rest[..., 9 * a:]
    return jnp.concatenate([qd_kd, r, k, v, za, zb, zc, gb_gc_u, vd, gm, low], axis=-1)


def kernel(x, c, ctx, c_ctx, w_mod, b_mod, norm_g, w_in, decay_w0, decay_up, iclr_a0, iclr_up, k_k, k_a, r_k,
           gn_g, gn_b, conv_w, conv_b, qk_norm_g, lambda_qk, subln_g, w_branch, w_out):
    bsz, n_lat, d = x.shape
    n_ctx = ctx.shape[1]
    depth = w_mod.shape[0]
    n_all = n_ctx + n_lat
    assert n_ctx % ROW_TILE == 0 and n_lat % ROW_TILE == 0 and n_lat % GRID_W == 0
    tiles_per_batch = n_all // ROW_TILE
    ctx_tiles = n_ctx // ROW_TILE
    da = decay_w0.shape[-1]
    rank = decay_up.shape[-2]

    n_cond = 16
    assert bsz < n_cond
    cond = jnp.zeros((n_cond, d), F32).at[:bsz].set(c).at[bsz].set(c_ctx)
    mod = _modulation(cond, w_mod, b_mod).reshape(depth, n_cond, 1, 3 * d)

    def mod_row(i):
        return jnp.where(i % tiles_per_batch < ctx_tiles, bsz, i // tiles_per_batch)

    xc, xl = ctx, x
    w_in_p = _permute_w_in(w_in.astype(BF16))
    cos_t, sin_t = _rope_tables(n_ctx, n_lat, BRANCH_W)
    gi = lax.broadcasted_iota(jnp.int32, (HALF_W, HALF_W), 0) // HEAD_A
    gj = lax.broadcasted_iota(jnp.int32, (HALF_W, HALF_W), 1) // HEAD_A
    ones_bd = (gi == gj).astype(BF16)
    dup = jnp.zeros((depth, 2, 4 * rank, da), F32)
    aup = jnp.zeros((depth, 2, 4 * rank, da), F32)
    for dd in range(2):
        dup = dup.at[:, dd, dd * rank:(dd + 1) * rank].set(decay_up[:, dd])
        aup = aup.at[:, dd, (2 + dd) * rank:(3 + dd) * rank].set(iclr_up[:, dd])
    dup, aup = dup.astype(BF16), aup.astype(BF16)
    w_branch_b, w_out_b = w_branch.astype(BF16), w_out.astype(BF16)
    n_rep = BRANCH_W // DH_C

    for l in range(depth):
        lam_init = jnp.full((1, 1), 0.8 - 0.6 * math.exp(-0.3 * l), F32)
        p, qh, kh, kk = _in_proj(xc, xl, mod[l], norm_g[l].reshape(1, d), w_in_p[l], cos_t, sin_t,
                                 jnp.tile(qk_norm_g[l, 0], n_rep).reshape(1, BRANCH_W),
                                 jnp.tile(qk_norm_g[l, 1], n_rep).reshape(1, BRANCH_W), k_k[l].reshape(1, da),
                                 r_k[l].reshape(1, da), ones_bd, mod_row, tiles_per_batch, ctx_tiles)
        yf, yb = _rwkv(p.reshape(bsz, n_all, -1), kk.reshape(bsz, n_all, da), decay_w0[l].reshape(2, 1, da), dup[l],
                       iclr_a0[l].reshape(2, 1, da), aup[l], k_a[l].reshape(1, da), bsz, n_ctx // CHUNK,
                       n_lat // CHUNK)
        yf, yb = yf.reshape(bsz * n_all, da), yb.reshape(bsz * n_all, da)
        yc = _attn(qh, kh, p, qk_norm_g[l], lambda_qk[l], subln_g[l].reshape(1, DV_C), lam_init, bsz, n_ctx, n_all)
        xc, xl = _merge(xc, xl, mod[l], yf, yb, p, yc, gn_g[l].reshape(1, da),
                        gn_b[l].reshape(1, da), conv_w[l], conv_b[l].reshape(1, da), w_branch_b[l], w_out_b[l],
                        ones_bd, mod_row, tiles_per_batch, ctx_tiles)
    return xl
```
